```python
import math
import jax, jax.numpy as jnp
from jax import lax
import numpy as np

D_MODEL = 2048
BATCH = 8
SEQ = 2048
DEPTH = 2

M_WIDTH = D_MODEL // 4
G_WIDTH = D_MODEL // 4
A_WIDTH = D_MODEL // 2
M_HEADS = 4
M_V = M_WIDTH // M_HEADS
M_QK = M_V // 2
M_CHUNK = 128
CONV_W = 4
G_HEADS = 4
G_V = G_WIDTH // G_HEADS
G_QK = G_V // 2
G_RANK = 16
G_TAU = 16.0
G_CHUNK = 64
A_HD = 64
A_HEADS = A_WIDTH // A_HD
A_KV = A_HEADS // 8
WINDOW = 128
A_BLOCK = WINDOW
N_BUCKETS = 32
MAX_DIST = 128
D_FF = ((8 * D_MODEL // 3 + 255) // 256) * 256
N_EXPERTS = 8
TOP_K = 2
D_FF_E = 7 * D_MODEL // 2
EPS = 1e-6

IN_SPLITS = [
    M_HEADS * M_QK, M_HEADS * M_QK, M_WIDTH, M_WIDTH, 2 * M_HEADS,
    G_HEADS * G_QK, G_HEADS * G_QK, G_WIDTH, G_WIDTH, G_RANK,
    A_HEADS * A_HD, A_KV * A_HD, A_KV * A_HD,
]
N_IN = sum(IN_SPLITS)

kernel_name = "hybrid_mlstm_gla_swa_moe_block"


def _rmsnorm(x, g):
    x32 = x.astype(jnp.float32)
    y = x32 * lax.rsqrt(jnp.mean(x32 * x32, axis=-1, keepdims=True) + EPS)
    return (y * g.astype(jnp.float32)).astype(x.dtype)


def _causal_dwconv(x, w, b):
    ch = x.shape[-1]
    y = lax.conv_general_dilated(x, w[:, None, :].astype(x.dtype), window_strides=(1,),
                                 padding=[(CONV_W - 1, 0)],
                                 dimension_numbers=('NWC', 'WIO', 'NWC'),
                                 feature_group_count=ch)
    return y + b


def _to_chunks(t, L):
    B, S = t.shape[:2]
    t = t.reshape((B, S // L, L) + t.shape[2:])
    return jnp.moveaxis(t, (1, 3), (0, 2))


def _from_chunks(t):
    t = jnp.moveaxis(t, (0, 2), (1, 3))
    return t.reshape((t.shape[0], t.shape[1] * t.shape[2]) + t.shape[3:])


def _mlstm(q, k, v, i_pre, log_f):
    B, S, H, dk = q.shape
    dv = v.shape[-1]
    L = M_CHUNK
    f32 = jnp.float32
    causal = jnp.asarray(np.tril(np.ones((L, L), dtype=bool)))

    def step(carry, inp):
        C, n, m = carry
        qc, kc, vc, ic, lfc = inp
        b = jnp.cumsum(lfc, axis=-1)
        log_d = jnp.where(causal, b[..., :, None] - b[..., None, :] + ic[..., None, :], -jnp.inf)
        log_inter = b + m[..., None]
        m_row = jnp.maximum(log_inter, jnp.max(log_d, axis=-1))
        d_mat = jnp.exp(log_d - m_row[..., None])
        w_inter = jnp.exp(log_inter - m_row)
        s = jnp.einsum('bhjk,bhsk->bhjs', qc, kc) * d_mat
        num = (w_inter[..., None] * jnp.einsum('bhjk,bhvk->bhjv', qc, C)
               + jnp.einsum('bhjs,bhsv->bhjv', s, vc))
        den = w_inter * jnp.einsum('bhjk,bhk->bhj', qc, n) + jnp.sum(s, axis=-1)
        h = num / jnp.maximum(jnp.abs(den), jnp.exp(-m_row))[..., None]
        b_last = b[..., -1]
        log_w = b_last[..., None] - b + ic
        m_new = jnp.maximum(b_last + m, jnp.max(log_w, axis=-1))
        w = jnp.exp(log_w - m_new[..., None])
        decay = jnp.exp(b_last + m - m_new)
        C = decay[..., None, None] * C + jnp.einsum('bhsv,bhsk->bhvk', vc * w[..., None], kc)
        n = decay[..., None] * n + jnp.einsum('bhs,bhsk->bhk', w, kc)
        return (C, n, m_new), h

    init = (jnp.zeros((B, H, dv, dk), f32), jnp.zeros((B, H, dk), f32), jnp.zeros((B, H), f32))
    xs = tuple(_to_chunks(t.astype(f32), L) for t in (q, k, v, i_pre, log_f))
    _, h = lax.scan(step, init, xs)
    return _from_chunks(h)


def _gla(q, k, v, log_a):
    B, S, H, dk = q.shape
    dv = v.shape[-1]
    L = G_CHUNK
    f32 = jnp.float32
    causal = jnp.asarray(np.tril(np.ones((L, L), dtype=bool)))

    def step(state, inp):
        qc, kc, vc, lac = inp
        b = jnp.cumsum(lac, axis=2)
        o_inter = jnp.einsum('bhjk,bhkv->bhjv', qc * jnp.exp(b), state)
        rel = jnp.where(causal[..., None], b[:, :, :, None, :] - b[:, :, None, :, :], -jnp.inf)
        att = jnp.einsum('bhjsk,bhjk->bhjs', kc[:, :, None, :, :] * jnp.exp(rel), qc)
        o_intra = jnp.einsum('bhjs,bhsv->bhjv', att, vc)
        b_last = b[:, :, -1]
        state = (jnp.exp(b_last)[..., None] * state
                 + jnp.einsum('bhsk,bhsv->bhkv', kc * jnp.exp(b_last[:, :, None, :] - b), vc))
        return state, o_inter + o_intra

    init = jnp.zeros((B, H, dk, dv), f32)
    xs = tuple(_to_chunks(t.astype(f32), L) for t in (q, k, v, log_a))
    _, o = lax.scan(step, init, xs)
    return _from_chunks(o)


def _t5_bucket(dist):
    max_exact = N_BUCKETS // 2
    d = np.maximum(dist, 1).astype(np.float32)
    large = max_exact + (np.log(d / max_exact) / math.log(MAX_DIST / max_exact)
                         * (N_BUCKETS - max_exact)).astype(np.int32)
    large = np.minimum(large, N_BUCKETS - 1)
    return np.where(dist < max_exact, dist, large).astype(np.int32)


def _swa_sinks(q, k, v, g_q, g_k, sinks, rel_bias):
    f32 = jnp.float32
    B, S, HQ, d = q.shape
    HKV = k.shape[2]
    G = HQ // HKV
    T = A_BLOCK
    nb = S // T
    q = _rmsnorm(q, g_q).astype(f32)
    k = _rmsnorm(k, g_k).astype(f32)
    v = v.astype(f32)
    qb = q.reshape(B, nb, T, HKV, G, d)

    def band(t):
        tp = jnp.pad(t, ((0, 0), (T, 0), (0, 0), (0, 0)))[:, :S]
        return jnp.concatenate([tp.reshape(B, nb, T, HKV, d), t.reshape(B, nb, T, HKV, d)], axis=2)

    kb, vb = band(k), band(v)
    j = np.arange(T)[:, None]
    s = np.arange(2 * T)[None, :]
    dist = j + T - s
    valid = (dist >= 0) & (dist < WINDOW)
    key_ok = (np.arange(nb)[:, None, None] * T - T + s[None]) >= 0
    mask = jnp.asarray(valid[None] & key_ok)
    buckets = _t5_bucket(np.clip(dist, 0, None))
    bias = jnp.transpose(rel_bias.astype(f32)[buckets], (2, 0, 1)).reshape(HKV, G, T, 2 * T)
    scores = jnp.einsum('bnjhgd,bnshd->bnhgjs', qb, kb) * (d ** -0.5) + bias
    scores = jnp.where(mask[None, :, None, None], scores, -jnp.inf)
    sink = sinks.astype(f32).reshape(HKV, G)[None, None, :, :, None]
    m = jnp.maximum(jnp.max(scores, axis=-1), sink)
    p = jnp.exp(scores - m[..., None])
    denom = jnp.sum(p, axis=-1) + jnp.exp(sink - m)
    out = jnp.einsum('bnhgjs,bnshd->bnjhgd', p, vb) / jnp.transpose(denom, (0, 1, 4, 2, 3))[..., None]
    return out.reshape(B, S, HQ * d)


def _hybrid_mixer(h, w_in, b_gates_m, w_conv_m, b_conv_m, g_out_m, w_gla_decay, b_gla_decay,
                  g_out_g, g_qnorm, g_knorm, sinks, rel_bias, w_out):
    B, S, _ = h.shape
    dt = h.dtype
    proj = h @ w_in
    (mq, mk, mv, mo, mif, gq, gk, gv, gg, ga, aq, ak, av) = jnp.split(
        proj, np.cumsum(IN_SPLITS)[:-1].tolist(), axis=-1)

    qk = jax.nn.silu(_causal_dwconv(jnp.concatenate([mq, mk], axis=-1), w_conv_m, b_conv_m))
    mq, mk = jnp.split(qk, 2, axis=-1)
    mq = mq.reshape(B, S, M_HEADS, M_QK)
    mk = mk.reshape(B, S, M_HEADS, M_QK) * (M_QK ** -0.5)
    mv = mv.reshape(B, S, M_HEADS, M_V)
    mif = mif.astype(jnp.float32)
    i_pre = mif[..., :M_HEADS] + b_gates_m[0].astype(jnp.float32)
    log_f = jax.nn.log_sigmoid(mif[..., M_HEADS:] + b_gates_m[1].astype(jnp.float32))
    hm = _mlstm(mq, mk, mv, i_pre, log_f)
    hm = _rmsnorm(hm, g_out_m.reshape(M_HEADS, M_V)).reshape(B, S, M_WIDTH)
    y_m = (hm * jax.nn.sigmoid(mo.astype(jnp.float32))).astype(dt)

    log_a = jax.nn.log_sigmoid((ga @ w_gla_decay + b_gla_decay).astype(jnp.float32)) / G_TAU
    og = _gla((gq * (G_QK ** -0.5)).reshape(B, S, G_HEADS, G_QK),
              gk.reshape(B, S, G_HEADS, G_QK),
              gv.reshape(B, S, G_HEADS, G_V),
              log_a.reshape(B, S, G_HEADS, G_QK))
    og = _rmsnorm(og, g_out_g.reshape(G_HEADS, G_V)).reshape(B, S, G_WIDTH)
    y_g = (og * jax.nn.silu(gg.astype(jnp.float32))).astype(dt)

    y_a = _swa_sinks(aq.reshape(B, S, A_HEADS, A_HD), ak.reshape(B, S, A_KV, A_HD),
                     av.reshape(B, S, A_KV, A_HD), g_qnorm, g_knorm, sinks, rel_bias).astype(dt)

    return jnp.concatenate([y_m, y_g, y_a], axis=-1) @ w_out


def _swiglu(h, wg, wu, wd):
    return (jax.nn.silu(h @ wg) * (h @ wu)) @ wd


def _moe_swiglu(h, w_router, w_gate, w_up, w_down):
    B, S, D = h.shape
    t = h.reshape(B * S, D)
    logits = (t @ w_router).astype(jnp.float32)
    top_val, top_idx = lax.top_k(logits, TOP_K)
    top_w = jax.nn.softmax(top_val, axis=-1)
    gates = jnp.sum(jax.nn.one_hot(top_idx, N_EXPERTS, dtype=jnp.float32) * top_w[..., None], axis=1)
    gates = gates.astype(t.dtype)
    out = jnp.zeros_like(t)
    for e in range(N_EXPERTS):
        out = out + gates[:, e:e + 1] * _swiglu(t, w_gate[e], w_up[e], w_down[e])
    return out.reshape(B, S, D)


def setup_inputs(seed: int = 0) -> dict:
    key = jax.random.key(seed)
    ks = jax.random.split(key, 32)
    f32 = jnp.float32
    n_even = (DEPTH + 1) // 2
    n_odd = DEPTH // 2

    def nrm(k, shape, scale):
        return jax.random.normal(k, shape, f32) * scale

    return {
        "x": nrm(ks[0], (BATCH, SEQ, D_MODEL), 1.0),
        "c": nrm(ks[1], (BATCH, D_MODEL), 1.0),
        "w_ada": nrm(ks[2], (DEPTH, D_MODEL, 6 * D_MODEL), D_MODEL ** -0.5),
        "b_ada": nrm(ks[3], (DEPTH, 6 * D_MODEL), 0.02),
        "g_mix_norm": 1.0 + nrm(ks[4], (DEPTH, D_MODEL), 0.02),
        "g_ffn_norm": 1.0 + nrm(ks[5], (DEPTH, D_MODEL), 0.02),
        "w_in": nrm(ks[6], (DEPTH, D_MODEL, N_IN), D_MODEL ** -0.5),
        "b_gates_m": jnp.stack([nrm(ks[7], (DEPTH, M_HEADS), 0.1),
                                3.0 + nrm(ks[8], (DEPTH, M_HEADS), 0.1)], axis=1),
        "w_conv_m": nrm(ks[9], (DEPTH, CONV_W, 2 * M_HEADS * M_QK), CONV_W ** -0.5),
        "b_conv_m": nrm(ks[10], (DEPTH, 2 * M_HEADS * M_QK), 0.02),
        "g_out_m": 1.0 + nrm(ks[11], (DEPTH, M_WIDTH), 0.02),
        "w_gla_decay": nrm(ks[12], (DEPTH, G_RANK, G_HEADS * G_QK), G_RANK ** -0.5),
        "b_gla_decay": nrm(ks[13], (DEPTH, G_HEADS * G_QK), 0.1),
        "g_out_g": 1.0 + nrm(ks[14], (DEPTH, G_WIDTH), 0.02),
        "g_qnorm": 1.0 + nrm(ks[15], (DEPTH, A_HD), 0.02),
        "g_knorm": 1.0 + nrm(ks[16], (DEPTH, A_HD), 0.02),
        "sinks": nrm(ks[17], (DEPTH, A_HEADS), 0.5),
        "rel_bias": nrm(ks[18], (N_BUCKETS, A_HEADS), 0.5),
        "w_out": nrm(ks[19], (DEPTH, D_MODEL, D_MODEL), D_MODEL ** -0.5),
        "w_ffn_gate": nrm(ks[20], (n_even, D_MODEL, D_FF), D_MODEL ** -0.5),
        "w_ffn_up": nrm(ks[21], (n_even, D_MODEL, D_FF), D_MODEL ** -0.5),
        "w_ffn_down": nrm(ks[22], (n_even, D_FF, D_MODEL), D_FF ** -0.5),
        "w_router": nrm(ks[23], (n_odd, D_MODEL, N_EXPERTS), D_MODEL ** -0.5),
        "w_moe_gate": nrm(ks[24], (n_odd, N_EXPERTS, D_MODEL, D_FF_E), D_MODEL ** -0.5),
        "w_moe_up": nrm(ks[25], (n_odd, N_EXPERTS, D_MODEL, D_FF_E), D_MODEL ** -0.5),
        "w_moe_down": nrm(ks[26], (n_odd, N_EXPERTS, D_FF_E, D_MODEL), D_FF_E ** -0.5),
    }


def reference(x, c, w_ada, b_ada, g_mix_norm, g_ffn_norm, w_in, b_gates_m, w_conv_m, b_conv_m,
              g_out_m, w_gla_decay, b_gla_decay, g_out_g, g_qnorm, g_knorm, sinks, rel_bias,
              w_out, w_ffn_gate, w_ffn_up, w_ffn_down, w_router, w_moe_gate, w_moe_up, w_moe_down):
    cond = jax.nn.silu(c)
    for l in range(DEPTH):
        mod = cond @ w_ada[l] + b_ada[l]
        sh_a, sc_a, gt_a, sh_f, sc_f, gt_f = jnp.split(mod[:, None, :], 6, axis=-1)
        h = _rmsnorm(x, g_mix_norm[l]) * (1 + sc_a) + sh_a
        x = x + gt_a * _hybrid_mixer(h, w_in[l], b_gates_m[l], w_conv_m[l], b_conv_m[l], g_out_m[l],
                                     w_gla_decay[l], b_gla_decay[l], g_out_g[l], g_qnorm[l],
                                     g_knorm[l], sinks[l], rel_bias, w_out[l])
        h = _rmsnorm(x, g_ffn_norm[l]) * (1 + sc_f) + sh_f
        if l % 2 == 0:
            y = _swiglu(h, w_ffn_gate[l // 2], w_ffn_up[l // 2], w_ffn_down[l // 2])
        else:
            y = _moe_swiglu(h, w_router[l // 2], w_moe_gate[l // 2], w_moe_up[l // 2], w_moe_down[l // 2])
        x = x + gt_f * y
    return x
```

```python
import functools

import numpy as np
import jax
import jax.numpy as jnp
from jax import lax
from jax.experimental import pallas as pl
from jax.experimental.pallas import tpu as pltpu

F32 = jnp.float32
BF16 = jnp.bfloat16
I32 = jnp.int32

M_HEADS = 4
M_QK = 64
M_V = 128
CONV_W = 4
G_HEADS = 4
G_QK = 64
G_V = 128
G_RANK = 16
G_TAU = 16.0
G_CHUNK = 64
A_HD = 64
A_HEADS = 16
A_KV = 2
WINDOW = 128
N_BUCKETS = 32
MAX_DIST = 128
N_EXPERTS = 8
EPS = 1e-6

LANES = 128
BLK = 128
VMEM_LIMIT = 56 * 1024 * 1024

C_MQK, C_MV, C_MO = 0, 512, 1024
C_GQK, C_GV, C_GG = 1536, 2048, 2560
C_AQ, C_AK, C_AV = 3072, 4096, 4224
N_MAIN = 4352
S_GA = 8


def _cparams(sem):
    return pltpu.CompilerParams(dimension_semantics=sem, vmem_limit_bytes=VMEM_LIMIT)


def _dot(a, b):
    return jnp.dot(a, b, preferred_element_type=F32)


def _dot_nt(a, b):
    return lax.dot_general(a, b, (((1,), (1,)), ((), ())), preferred_element_type=F32)


def _split_dot(a, b_f32):
    hi = b_f32.astype(BF16)
    lo = (b_f32 - hi.astype(F32)).astype(BF16)
    return _dot(a, hi) + _dot(a, lo)


def _sigmoid(x):
    return 1.0 / (1.0 + jnp.exp(-x))


def _log_sigmoid(x):
    return jnp.minimum(x, 0.0) - jnp.log(1.0 + jnp.exp(-jnp.abs(x)))


def _norm_mod(x, g, sc, sh):
    ms = jnp.mean(x * x, axis=-1, keepdims=True)
    return (x * lax.rsqrt(ms + EPS) * g) * (1.0 + sc) + sh


def _ada_kernel(c_ref, w_ref, b_ref, o_ref):
    c = c_ref[...]
    cond = (c * _sigmoid(c)).astype(BF16)
    o_ref[...] = _dot(cond, w_ref[...].astype(BF16)) + b_ref[...]


def _ada(c, w_ada, b_ada):
    depth, d, n6 = w_ada.shape
    bsz = c.shape[0]
    tn = 1024
    return pl.pallas_call(
        _ada_kernel,
        grid=(depth, n6 // tn),
        in_specs=[
            pl.BlockSpec((bsz, d), lambda l, n: (0, 0)),
            pl.BlockSpec((None, d, tn), lambda l, n: (l, 0, n)),
            pl.BlockSpec((None, 1, tn), lambda l, n: (l, 0, n)),
        ],
        out_specs=pl.BlockSpec((None, bsz, tn), lambda l, n: (l, 0, n)),
        out_shape=jax.ShapeDtypeStruct((depth, bsz, n6), F32),
        compiler_params=_cparams(("arbitrary", "arbitrary")),
        name="ada_mod",
    )(c, w_ada, b_ada.reshape(depth, 1, n6))


def _inproj_kernel(x_ref, g_ref, sc_ref, sh_ref, wm_ref, ws_ref, om_ref, os_ref, h_ref):
    @pl.when(pl.program_id(1) == 0)
    def _():
        h = _norm_mod(x_ref[...], g_ref[...], sc_ref[...], sh_ref[...]).astype(BF16)
        h_ref[...] = h
        os_ref[...] = _dot(h, ws_ref[...])

    om_ref[...] = _dot(h_ref[...], wm_ref[...]).astype(BF16)


def _inproj(x2, g, mod4, w_main, w_small, seq):
    n, d = x2.shape
    tm, tn = 1024, N_MAIN // 2
    per_b = seq // tm
    mod_spec = lambda k: pl.BlockSpec((None, None, 1, d), lambda i, j: (i // per_b, k, 0, 0))
    return pl.pallas_call(
        _inproj_kernel,
        grid=(n // tm, N_MAIN // tn),
        in_specs=[
            pl.BlockSpec((tm, d), lambda i, j: (i, 0)),
            pl.BlockSpec((1, d), lambda i, j: (0, 0)),
            mod_spec(1), mod_spec(0),
            pl.BlockSpec((d, tn), lambda i, j: (0, j)),
            pl.BlockSpec((d, LANES), lambda i, j: (0, 0)),
        ],
        out_specs=[
            pl.BlockSpec((tm, tn), lambda i, j: (i, j)),
            pl.BlockSpec((tm, LANES), lambda i, j: (i, 0)),
        ],
        out_shape=[jax.ShapeDtypeStruct((n, N_MAIN), BF16),
                   jax.ShapeDtypeStruct((n, LANES), F32)],
        scratch_shapes=[pltpu.VMEM((tm, d), BF16)],
        compiler_params=_cparams(("arbitrary", "arbitrary")),
        name="in_proj",
    )(x2, g, mod4, mod4, w_main, w_small)


def _mlstm_kernel(qk_ref, v_ref, o_ref, sm_ref, wc_ref, bc_ref, gb_ref, go_ref, y_ref,
                  tail_ref, ct_ref, n_ref, m_ref):
    L = BLK
    nqk = M_HEADS * M_QK

    @pl.when(pl.program_id(1) == 0)
    def _():
        tail_ref[...] = jnp.zeros_like(tail_ref)
        ct_ref[...] = jnp.zeros_like(ct_ref)
        n_ref[...] = jnp.zeros_like(n_ref)
        m_ref[...] = jnp.zeros_like(m_ref)

    x = qk_ref[...].astype(F32)
    prev = tail_ref[...]
    row8 = lax.broadcasted_iota(I32, prev.shape, 0)
    acc = x * wc_ref[CONV_W - 1:CONV_W, :] + bc_ref[...]
    for dlt in range(1, CONV_W):
        xs = pltpu.roll(x, dlt, axis=0)
        ps = pltpu.roll(prev, dlt, axis=0)
        top = jnp.where(row8 < dlt, ps, xs[0:8])
        xs = jnp.concatenate([top, xs[8:]], axis=0)
        acc = acc + xs * wc_ref[CONV_W - 1 - dlt:CONV_W - dlt, :]
    tail_ref[...] = x[L - 8:L]
    qk = acc * _sigmoid(acc)
    q = qk[:, :nqk]
    k = qk[:, nqk:] * (M_QK ** -0.5)
    kt_b = k.T.astype(BF16)

    gates = sm_ref[...] + gb_ref[...]
    row = lax.broadcasted_iota(I32, (L, L), 0)
    col = lax.broadcasted_iota(I32, (L, L), 1)
    causal = row >= col
    tri = causal.astype(BF16)
    bcum = _split_dot(tri, _log_sigmoid(gates))
    z = jnp.where(col < M_HEADS, gates, bcum)
    zt = z.T
    lane_q = lax.broadcasted_iota(I32, (L, nqk), 1)

    outs = []
    for h in range(M_HEADS):
        in_head = (lane_q >= h * M_QK) & (lane_q < (h + 1) * M_QK)
        qm = jnp.where(in_head, q, 0.0)
        qm_b = qm.astype(BF16)
        v_h = v_ref[:, h * M_V:(h + 1) * M_V]
        m_prev = m_ref[h:h + 1, 0:1]
        n_h = n_ref[h:h + 1, :]
        ct_h = ct_ref[h]
        bcol = z[:, M_HEADS + h:M_HEADS + h + 1]
        icol = z[:, h:h + 1]
        brow = zt[M_HEADS + h:M_HEADS + h + 1, :]
        irow = zt[h:h + 1, :]
        logd = jnp.where(causal, bcol - brow + irow, -jnp.inf)
        log_inter = bcol + m_prev
        m_row = jnp.maximum(log_inter, jnp.max(logd, axis=1, keepdims=True))
        dmat = jnp.exp(logd - m_row)
        w_inter = jnp.exp(log_inter - m_row)
        s = _dot(qm_b, kt_b) * dmat
        num = w_inter * _dot(qm_b, ct_h.astype(BF16)) + _dot(s.astype(BF16), v_h)
        den = (w_inter * jnp.sum(qm * n_h, axis=1, keepdims=True)
               + jnp.sum(s, axis=1, keepdims=True))
        outs.append(num / jnp.maximum(jnp.abs(den), jnp.exp(-m_row)))
        b_last = bcol[L - 1:L, :]
        log_w = b_last - bcol + icol
        m_new = jnp.maximum(b_last + m_prev, jnp.max(log_w, axis=0, keepdims=True))
        w = jnp.exp(log_w - m_new)
        decay = jnp.exp(b_last + m_prev - m_new)
        vw = (v_h.astype(F32) * w).astype(BF16)
        ct_ref[h] = decay * ct_h + _dot(kt_b, vw)
        n_ref[h:h + 1, :] = decay * n_h + jnp.sum(k * w, axis=0, keepdims=True)
        m_ref[h:h + 1, :] = jnp.broadcast_to(m_new, (1, LANES))

    og = _sigmoid(o_ref[...].astype(F32))
    for h in range(M_HEADS):
        hh = outs[h]
        ms = jnp.mean(hh * hh, axis=1, keepdims=True)
        sl = slice(h * M_V, (h + 1) * M_V)
        y = hh * lax.rsqrt(ms + EPS) * go_ref[:, sl]
        y_ref[:, sl] = (y * og[:, sl]).astype(BF16)


def _mlstm(pm, ps, wc, bc, gb, go, bsz, seq):
    nb = seq // BLK
    wd = M_HEADS * M_V
    rows = lambda cb: pl.BlockSpec((BLK, wd), lambda b, c: (b * nb + c, cb))
    full = lambda shp: pl.BlockSpec(shp, lambda b, c: (0,) * len(shp))
    return pl.pallas_call(
        _mlstm_kernel,
        grid=(bsz, nb),
        in_specs=[
            rows(C_MQK // wd), rows(C_MV // wd), rows(C_MO // wd),
            pl.BlockSpec((BLK, LANES), lambda b, c: (b * nb + c, 0)),
            full((CONV_W, 2 * M_HEADS * M_QK)), full((1, 2 * M_HEADS * M_QK)),
            full((1, LANES)), full((1, wd)),
        ],
        out_specs=pl.BlockSpec((BLK, wd), lambda b, c: (b * nb + c, 0)),
        out_shape=jax.ShapeDtypeStruct((bsz * seq, wd), BF16),
        scratch_shapes=[
            pltpu.VMEM((8, 2 * M_HEADS * M_QK), F32),
            pltpu.VMEM((M_HEADS, M_HEADS * M_QK, M_V), F32),
            pltpu.VMEM((8, M_HEADS * M_QK), F32),
            pltpu.VMEM((8, LANES), F32),
        ],
        compiler_params=_cparams(("arbitrary", "arbitrary")),
        name="mlstm",
    )(pm, pm, pm, ps, wc, bc, gb, go)


def _gla_kernel(qk_ref, v_ref, g_ref, sm_ref, wd_ref, bd_ref, go_ref, y_ref,
                s_ref, b_ref, lhs_ref, r_ref):
    L, C = BLK, G_CHUNK
    nqk = G_HEADS * G_QK
    pair_k = 2 * G_QK
    pair_v = 2 * G_V
    n_pairs = G_HEADS // 2

    @pl.when(pl.program_id(1) == 0)
    def _():
        s_ref[...] = jnp.zeros_like(s_ref)

    zdec = _dot(sm_ref[...].astype(BF16), wd_ref[...].astype(BF16)) + bd_ref[...]
    log_a = _log_sigmoid(zdec) / G_TAU
    row = lax.broadcasted_iota(I32, (L, L), 0)
    col = lax.broadcasted_iota(I32, (L, L), 1)
    tri = ((row >= col) & ((row // C) == (col // C))).astype(BF16)
    b = _split_dot(tri, log_a)
    b_ref[...] = b
    bt = b.T

    q = qk_ref[:, :nqk].astype(F32) * (G_QK ** -0.5)
    k = qk_ref[:, nqk:].astype(F32)
    kt_b = k.T.astype(BF16)
    b_end = jnp.concatenate(
        [jnp.broadcast_to(b[(cc + 1) * C - 1:(cc + 1) * C, :], (C, nqk)) for cc in range(L // C)], axis=0)
    kdt = (k * jnp.exp(b_end - b)).T
    qe = q * jnp.exp(b)

    lane_s = lax.broadcasted_iota(I32, (pair_k, L), 1)
    lane_k = lax.broadcasted_iota(I32, (C, pair_k), 1)
    rblk = lax.broadcasted_iota(I32, (pair_k, pair_v), 0) // G_QK
    cblk = lax.broadcasted_iota(I32, (pair_k, pair_v), 1) // G_V
    blockdiag = rblk == cblk
    row_c = lax.broadcasted_iota(I32, (C, L), 0)
    col_c = lax.broadcasted_iota(I32, (C, L), 1)

    pair_out = [[None] * (L // C) for _ in range(n_pairs)]
    for cc in range(L // C):
        r0 = cc * C
        for p in range(n_pairs):
            kl = slice(p * pair_k, (p + 1) * pair_k)
            vl = slice(p * pair_v, (p + 1) * pair_v)
            v_p = v_ref[:, vl]
            s_p = s_ref[p]
            o = _dot(qe[r0:r0 + C, kl].astype(BF16), s_p.astype(BF16))

            bq = b[r0:r0 + C, kl]
            qh = [jnp.where((lane_k // G_QK) == hh, q[r0:r0 + C, kl], 0.0) for hh in range(2)]

            def build(gi, carry, bq=bq, qh=qh, r0=r0, p=p):
                grp = b_ref[pl.ds(pl.multiple_of(r0 + gi * 8, 8), 8), p * pair_k:(p + 1) * pair_k]
                for i in range(8):
                    e = jnp.exp(jnp.minimum(bq - grp[i:i + 1, :], 0.0))
                    off = pl.multiple_of((gi * 8 + i) * C, C)
                    for hh in range(2):
                        lhs_ref[hh, pl.ds(off, C), :] = (qh[hh] * e).astype(BF16)
                return carry

            lax.fori_loop(0, C // 8, build, 0)
            kt_p = kt_b[kl, :]
            for hh in range(2):
                r_ref[hh] = _dot(lhs_ref[hh], kt_p)

            def reduce(si, att, r0=r0):
                off = pl.multiple_of(si * C, C)
                keep = (col_c == r0 + si) & (row_c >= si)
                return tuple(att[hh] + jnp.where(keep, r_ref[hh, pl.ds(off, C), :], 0.0)
                             for hh in range(2))

            att = lax.fori_loop(0, C, reduce,
                                (jnp.zeros((C, L), F32), jnp.zeros((C, L), F32)), unroll=4)
            intra = [_dot(att[hh].astype(BF16), v_p[:, hh * G_V:(hh + 1) * G_V]) for hh in range(2)]
            pair_out[p][cc] = o + jnp.concatenate(intra, axis=1)

            in_chunk = (lane_s >= r0) & (lane_s < r0 + C)
            kd = jnp.where(in_chunk, kdt[kl, :], 0.0).astype(BF16)
            upd = jnp.where(blockdiag, _dot(kd, v_p), 0.0)
            dcol = jnp.exp(bt[kl, r0 + C - 1:r0 + C])
            s_ref[p] = dcol * s_p + upd

    gate = g_ref[...].astype(F32)
    gate = gate * _sigmoid(gate)
    for p in range(n_pairs):
        o_p = jnp.concatenate(pair_out[p], axis=0)
        for hh in range(2):
            h = 2 * p + hh
            oh = o_p[:, hh * G_V:(hh + 1) * G_V]
            ms = jnp.mean(oh * oh, axis=1, keepdims=True)
            sl = slice(h * G_V, (h + 1) * G_V)
            y = oh * lax.rsqrt(ms + EPS) * go_ref[:, sl]
            y_ref[:, sl] = (y * gate[:, sl]).astype(BF16)


def _gla(pm, ps, wd, bd, go, bsz, seq):
    nb = seq // BLK
    w = G_HEADS * G_V
    nqk = G_HEADS * G_QK
    rows = lambda cb: pl.BlockSpec((BLK, w), lambda b, c: (b * nb + c, cb))
    full = lambda shp: pl.BlockSpec(shp, lambda b, c: (0,) * len(shp))
    return pl.pallas_call(
        _gla_kernel,
        grid=(bsz, nb),
        in_specs=[
            rows(C_GQK // w), rows(C_GV // w), rows(C_GG // w),
            pl.BlockSpec((BLK, LANES), lambda b, c: (b * nb + c, 0)),
            full((LANES, nqk)), full((1, nqk)), full((1, w)),
        ],
        out_specs=pl.BlockSpec((BLK, w), lambda b, c: (b * nb + c, 0)),
        out_shape=jax.ShapeDtypeStruct((bsz * seq, w), BF16),
        scratch_shapes=[
            pltpu.VMEM((G_HEADS // 2, 2 * G_QK, 2 * G_V), F32),
            pltpu.VMEM((BLK, nqk), F32),
            pltpu.VMEM((2, G_CHUNK * G_CHUNK, 2 * G_QK), BF16),
            pltpu.VMEM((2, G_CHUNK * G_CHUNK, BLK), F32),
        ],
        compiler_params=_cparams(("arbitrary", "arbitrary")),
        name="gla",
    )(pm, pm, pm, ps, wd, bd, go)


def _group_mean_sq(x, ind):
    return _split_dot_rhs(x * x, ind)


def _split_dot_rhs(a_f32, b_bf16):
    hi = a_f32.astype(BF16)
    lo = (a_f32 - hi.astype(F32)).astype(BF16)
    return _dot(hi, b_bf16) + _dot(lo, b_bf16)


def _swa_kernel(q_ref, kc_ref, kp_ref, vc_ref, vp_ref, bias_ref, gq_ref, gk_ref, sink_ref, ind_ref,
                y_ref):
    T = BLK
    heads_per_tile = LANES // A_HD
    n_tiles = A_HEADS // heads_per_tile
    grp = A_HEADS // A_KV
    first_key = jnp.where(pl.program_id(1) == 0, T, 0)
    ind = ind_ref[...]

    qn = []
    for t in range(n_tiles):
        sl = slice(t * LANES, (t + 1) * LANES)
        qt = q_ref[:, sl].astype(F32)
        ms = _group_mean_sq(qt, ind)
        qn.append((qt * lax.rsqrt(ms + EPS) * gq_ref[:, sl]).astype(BF16))
    kk = jnp.concatenate([kp_ref[...], kc_ref[...]], axis=0).astype(F32)
    kn = kk * lax.rsqrt(_group_mean_sq(kk, ind) + EPS) * gk_ref[...]
    vv = jnp.concatenate([vp_ref[...], vc_ref[...]], axis=0).astype(F32)
    k_sw = pltpu.roll(kn, A_HD, axis=1)
    v_sw = pltpu.roll(vv, A_HD, axis=1)
    half = lax.broadcasted_iota(I32, (2 * T, LANES), 1) // A_HD

    j = lax.broadcasted_iota(I32, (T, 2 * T), 0)
    s = lax.broadcasted_iota(I32, (T, 2 * T), 1)
    dist = j + T - s
    mask = (dist >= 0) & (dist < WINDOW) & (s >= first_key)

    acc = [None] * n_tiles
    for g in range(A_KV):
        for p in range(heads_per_tile):
            ksrc, vsrc = (kn, vv) if g == p else (k_sw, v_sw)
            kz = jnp.where(half == p, ksrc, 0.0).astype(BF16)
            vz = jnp.where(half == p, vsrc, 0.0).astype(BF16)
            heads = [g * grp + heads_per_tile * u + p for u in range(grp // heads_per_tile)]
            qs = jnp.concatenate([qn[h // heads_per_tile] for h in heads], axis=0)
            sc_all = _dot_nt(qs, kz)
            probs, dens = [], []
            for u, h in enumerate(heads):
                sc = jnp.where(mask, sc_all[u * T:(u + 1) * T] + bias_ref[h], -jnp.inf)
                sink = sink_ref[h]
                m = jnp.maximum(jnp.max(sc, axis=1, keepdims=True), sink)
                pr = jnp.exp(sc - m)
                dens.append(jnp.sum(pr, axis=1, keepdims=True) + jnp.exp(sink - m))
                probs.append(pr.astype(BF16))
            o_all = _dot(jnp.concatenate(probs, axis=0), vz)
            for u, h in enumerate(heads):
                o = o_all[u * T:(u + 1) * T] / dens[u]
                t = h // heads_per_tile
                acc[t] = o if acc[t] is None else acc[t] + o
    for t in range(n_tiles):
        y_ref[:, t * LANES:(t + 1) * LANES] = acc[t].astype(BF16)


def _swa(pm, bias, gq, gk, sinks, ind, bsz, seq):
    nb = seq // BLK
    wq = A_HEADS * A_HD
    cur = lambda cb: pl.BlockSpec((BLK, LANES), lambda b, c: (b * nb + c, cb))
    prv = lambda cb: pl.BlockSpec((BLK, LANES), lambda b, c: (b * nb + jnp.maximum(c - 1, 0), cb))
    full = lambda shp: pl.BlockSpec(shp, lambda b, c: (0,) * len(shp))
    return pl.pallas_call(
        _swa_kernel,
        grid=(bsz, nb),
        in_specs=[
            pl.BlockSpec((BLK, wq), lambda b, c: (b * nb + c, C_AQ // wq)),
            cur(C_AK // LANES), prv(C_AK // LANES), cur(C_AV // LANES), prv(C_AV // LANES),
            full((A_HEADS, BLK, 2 * BLK)), full((1, wq)), full((1, LANES)),
            pl.BlockSpec(memory_space=pltpu.SMEM),
            full((LANES, LANES)),
        ],
        out_specs=pl.BlockSpec((BLK, wq), lambda b, c: (b * nb + c, 0)),
        out_shape=jax.ShapeDtypeStruct((bsz * seq, wq), BF16),
        compiler_params=_cparams(("arbitrary", "arbitrary")),
        name="swa",
    )(pm, pm, pm, pm, pm, bias, gq, gk, sinks, ind)


def _outproj_kernel(ym_ref, yg_ref, ya_ref, w_ref, x_ref, gt_ref, o_ref):
    y = jnp.concatenate([ym_ref[...], yg_ref[...], ya_ref[...]], axis=1)
    o_ref[...] = x_ref[...] + gt_ref[...] * _dot(y, w_ref[...])


def _outproj(ym, yg, ya, w, x2, mod4, seq):
    n, d = x2.shape
    tm, tn = 1024, 1024
    per_b = seq // tm
    return pl.pallas_call(
        _outproj_kernel,
        grid=(n // tm, d // tn),
        in_specs=[
            pl.BlockSpec((tm, ym.shape[1]), lambda i, j: (i, 0)),
            pl.BlockSpec((tm, yg.shape[1]), lambda i, j: (i, 0)),
            pl.BlockSpec((tm, ya.shape[1]), lambda i, j: (i, 0)),
            pl.BlockSpec((d, tn), lambda i, j: (0, j)),
            pl.BlockSpec((tm, tn), lambda i, j: (i, j)),
            pl.BlockSpec((None, None, 1, tn), lambda i, j: (i // per_b, 2, 0, j)),
        ],
        out_specs=pl.BlockSpec((tm, tn), lambda i, j: (i, j)),
        out_shape=jax.ShapeDtypeStruct((n, d), F32),
        compiler_params=_cparams(("arbitrary", "arbitrary")),
        name="out_proj",
    )(ym, yg, ya, w, x2, mod4)


def _ffn_kernel(x_ref, g_ref, sc_ref, sh_ref, gt_ref, wg_ref, wu_ref, wd_ref, o_ref, h_ref):
    f = pl.program_id(1)

    @pl.when(f == 0)
    def _():
        h_ref[...] = _norm_mod(x_ref[...], g_ref[...], sc_ref[...], sh_ref[...]).astype(BF16)
        o_ref[...] = jnp.zeros_like(o_ref)

    h = h_ref[...]
    gate = _dot(h, wg_ref[...])
    a = (gate * _sigmoid(gate) * _dot(h, wu_ref[...])).astype(BF16)
    o_ref[...] += _dot(a, wd_ref[...])

    @pl.when(f == pl.num_programs(1) - 1)
    def _():
        o_ref[...] = x_ref[...] + gt_ref[...] * o_ref[...]


def _ffn(x2, g, mod4, wg, wu, wd, seq):
    n, d = x2.shape
    dff = wg.shape[1]
    tm, tf = 1024, 512
    per_b = seq // tm
    mod_spec = lambda k: pl.BlockSpec((None, None, 1, d), lambda i, f: (i // per_b, k, 0, 0))
    return pl.pallas_call(
        _ffn_kernel,
        grid=(n // tm, dff // tf),
        in_specs=[
            pl.BlockSpec((tm, d), lambda i, f: (i, 0), pipeline_mode=pl.Buffered(1)),
            pl.BlockSpec((1, d), lambda i, f: (0, 0)),
            mod_spec(4), mod_spec(3), mod_spec(5),
            pl.BlockSpec((d, tf), lambda i, f: (0, f)),
            pl.BlockSpec((d, tf), lambda i, f: (0, f)),
            pl.BlockSpec((tf, d), lambda i, f: (f, 0)),
        ],
        out_specs=pl.BlockSpec((tm, d), lambda i, f: (i, 0)),
        out_shape=jax.ShapeDtypeStruct((n, d), F32),
        scratch_shapes=[pltpu.VMEM((tm, d), BF16)],
        compiler_params=_cparams(("arbitrary", "arbitrary")),
        name="ffn_dense",
    )(x2, g, mod4, mod4, mod4, wg, wu, wd)


def _router_kernel(x_ref, g_ref, sc_ref, sh_ref, wh_ref, wl_ref, tri_ref, h_ref, meta_ref, cnt_ref,
                   run_ref):
    @pl.when(pl.program_id(0) == 0)
    def _():
        run_ref[...] = jnp.zeros_like(run_ref)

    h = _norm_mod(x_ref[...], g_ref[...], sc_ref[...], sh_ref[...])
    h_ref[...] = h
    h_hi = h.astype(BF16)
    h_lo = (h - h_hi.astype(F32)).astype(BF16)
    logits = _dot(h_hi, wh_ref[...]) + _dot(h_lo, wh_ref[...]) + _dot(h_hi, wl_ref[...])
    lane = lax.broadcasted_iota(I32, logits.shape, 1).astype(F32)
    l1 = jnp.where(lane < N_EXPERTS, logits, -jnp.inf)
    m1 = jnp.max(l1, axis=1, keepdims=True)
    i1 = jnp.min(jnp.where(l1 == m1, lane, float(LANES)), axis=1, keepdims=True)
    l2 = jnp.where(lane == i1, -jnp.inf, l1)
    m2 = jnp.max(l2, axis=1, keepdims=True)
    i2 = jnp.min(jnp.where(l2 == m2, lane, float(LANES)), axis=1, keepdims=True)
    e2 = jnp.exp(m2 - m1)
    w1 = 1.0 / (1.0 + e2)
    w2 = e2 / (1.0 + e2)
    member = jnp.where((lane == i1) | (lane == i2), 1.0, 0.0)
    run = run_ref[0:1, :]
    rank = _dot(tri_ref[...], member.astype(BF16)) + run
    p1 = jnp.sum(jnp.where(lane == i1, rank, 0.0), axis=1, keepdims=True)
    p2 = jnp.sum(jnp.where(lane == i2, rank, 0.0), axis=1, keepdims=True)
    run = run + jnp.sum(member, axis=0, keepdims=True)
    run_ref[0:1, :] = run
    cnt_ref[...] = jnp.broadcast_to(run, cnt_ref.shape)
    meta = jnp.zeros(logits.shape, F32)
    for idx, val in enumerate([i1, i2, w1, w2, p1, p2]):
        meta = jnp.where(lane == float(idx), val, meta)
    meta_ref[...] = meta


def _router(x2, g, mod4, wr_hi, wr_lo, seq):
    n, d = x2.shape
    tm = 512
    per_b = seq // tm
    tri = jnp.asarray(np.tril(np.ones((tm, tm), np.float32), -1), BF16)
    mod_spec = lambda k: pl.BlockSpec((None, None, 1, d), lambda i: (i // per_b, k, 0, 0))
    return pl.pallas_call(
        _router_kernel,
        grid=(n // tm,),
        in_specs=[
            pl.BlockSpec((tm, d), lambda i: (i, 0)),
            pl.BlockSpec((1, d), lambda i: (0, 0)),
            mod_spec(4), mod_spec(3),
            pl.BlockSpec((d, LANES), lambda i: (0, 0)),
            pl.BlockSpec((d, LANES), lambda i: (0, 0)),
            pl.BlockSpec((tm, tm), lambda i: (0, 0)),
        ],
        out_specs=[
            pl.BlockSpec((tm, d), lambda i: (i, 0)),
            pl.BlockSpec((tm, LANES), lambda i: (i, 0)),
            pl.BlockSpec((8, LANES), lambda i: (0, 0)),
        ],
        out_shape=[jax.ShapeDtypeStruct((n, d), F32),
                   jax.ShapeDtypeStruct((n, LANES), F32),
                   jax.ShapeDtypeStruct((8, LANES), F32)],
        scratch_shapes=[pltpu.VMEM((8, LANES), F32)],
        compiler_params=_cparams(("arbitrary",)),
        name="router",
    )(x2, g, mod4, mod4, wr_hi, wr_lo, tri)


MOE_TM = 1024


def _gather_rows(idx_ref, src_hbm, dst_ref, sem, n_rows, dst_off=0):
    def row_copy(r):
        return pltpu.make_async_copy(src_hbm.at[pl.ds(idx_ref[dst_off + r], 1), :],
                                     dst_ref.at[pl.ds(r, 1), :], sem)

    def start(r, c):
        row_copy(r).start()
        return c

    def wait(r, c):
        row_copy(r).wait()
        return c

    lax.fori_loop(0, n_rows, start, 0, unroll=8)
    lax.fori_loop(0, n_rows, wait, 0, unroll=8)


def _moe_kernel(te_ref, nu_ref, tok_hbm, h_hbm, wg_ref, wu_ref, wd_ref, o_ref,
                idx_ref, xf_ref, xb_ref, sem_i, sem_r):
    t = pl.program_id(0)
    f = pl.program_id(1)
    used = t < nu_ref[0]

    @pl.when(used & (f == 0))
    def _():
        cp = pltpu.make_async_copy(tok_hbm.at[pl.ds(pl.multiple_of(t * MOE_TM, MOE_TM), MOE_TM)],
                                   idx_ref, sem_i)
        cp.start()
        cp.wait()
        _gather_rows(idx_ref, h_hbm, xf_ref, sem_r, MOE_TM)
        xb_ref[...] = xf_ref[...].astype(BF16)
        o_ref[...] = jnp.zeros_like(o_ref)

    @pl.when(used)
    def _():
        x = xb_ref[...]
        gate = _dot(x, wg_ref[...])
        a = (gate * _sigmoid(gate) * _dot(x, wu_ref[...])).astype(BF16)
        o_ref[...] += _dot(a, wd_ref[...])

    @pl.when(jnp.logical_not(used) & (f == 0))
    def _():
        o_ref[...] = jnp.zeros_like(o_ref)


def _moe(tile_expert, n_used, src_tok, h, wg, wu, wd):
    n, d = h.shape
    n_tiles = tile_expert.shape[0]
    dff = wg.shape[2]
    tf = 512
    nf = dff // tf

    def f_blk(t, f, nu):
        return jnp.where(t < nu[0], f, nf - 1)

    grid_spec = pltpu.PrefetchScalarGridSpec(
        num_scalar_prefetch=2,
        grid=(n_tiles, nf),
        in_specs=[
            pl.BlockSpec(memory_space=pl.ANY),
            pl.BlockSpec(memory_space=pl.ANY),
            pl.BlockSpec((None, d, tf), lambda t, f, te, nu: (te[t], 0, f_blk(t, f, nu))),
            pl.BlockSpec((None, d, tf), lambda t, f, te, nu: (te[t], 0, f_blk(t, f, nu))),
            pl.BlockSpec((None, tf, d), lambda t, f, te, nu: (te[t], f_blk(t, f, nu), 0)),
        ],
        out_specs=pl.BlockSpec((MOE_TM, d), lambda t, f, te, nu: (t, 0)),
        scratch_shapes=[
            pltpu.SMEM((MOE_TM,), I32),
            pltpu.VMEM((MOE_TM, d), F32),
            pltpu.VMEM((MOE_TM, d), BF16),
            pltpu.SemaphoreType.DMA(()),
            pltpu.SemaphoreType.DMA(()),
        ],
    )
    return pl.pallas_call(
        _moe_kernel,
        grid_spec=grid_spec,
        out_shape=jax.ShapeDtypeStruct((n_tiles * MOE_TM, d), F32),
        compiler_params=_cparams(("arbitrary", "arbitrary")),
        name="moe_experts",
    )(tile_expert, n_used, src_tok, h, wg, wu, wd)


COMB_TM = 512


def _combine_kernel(dest_hbm, y_hbm, x_ref, meta_ref, gt_ref, o_ref, idx_ref, y0_ref, y1_ref, sem_i, sem_r):
    i = pl.program_id(0)
    cp = pltpu.make_async_copy(
        dest_hbm.at[pl.ds(pl.multiple_of(i * 2 * COMB_TM, 2 * COMB_TM), 2 * COMB_TM)], idx_ref, sem_i)
    cp.start()
    cp.wait()
    _gather_rows(idx_ref, y_hbm, y0_ref, sem_r, COMB_TM, 0)
    _gather_rows(idx_ref, y_hbm, y1_ref, sem_r, COMB_TM, COMB_TM)
    w1 = meta_ref[:, 2:3]
    w2 = meta_ref[:, 3:4]
    o_ref[...] = x_ref[...] + gt_ref[...] * (w1 * y0_ref[...] + w2 * y1_ref[...])


def _combine(dest, y_sorted, x2, meta, mod4, seq):
    n, d = x2.shape
    tm = COMB_TM
    per_b = seq // tm
    return pl.pallas_call(
        _combine_kernel,
        grid=(n // tm,),
        in_specs=[
            pl.BlockSpec(memory_space=pl.ANY),
            pl.BlockSpec(memory_space=pl.ANY),
            pl.BlockSpec((tm, d), lambda i: (i, 0)),
            pl.BlockSpec((tm, LANES), lambda i: (i, 0)),
            pl.BlockSpec((None, None, 1, d), lambda i: (i // per_b, 5, 0, 0)),
        ],
        out_specs=pl.BlockSpec((tm, d), lambda i: (i, 0)),
        out_shape=jax.ShapeDtypeStruct((n, d), F32),
        scratch_shapes=[
            pltpu.SMEM((2 * tm,), I32),
            pltpu.VMEM((tm, d), F32),
            pltpu.VMEM((tm, d), F32),
            pltpu.SemaphoreType.DMA(()),
            pltpu.SemaphoreType.DMA(()),
        ],
        compiler_params=_cparams(("arbitrary",)),
        name="moe_combine",
    )(dest, y_sorted, x2, meta, mod4)


def _moe_layer(x2, g, mod4, w_router, wg, wu, wd, seq):
    n, d = x2.shape
    wr = jnp.zeros((d, LANES), F32).at[:, :N_EXPERTS].set(w_router)
    wr_hi = wr.astype(BF16)
    wr_lo = (wr - wr_hi.astype(F32)).astype(BF16)
    h, meta, cnt = _router(x2, g, mod4, wr_hi, wr_lo, seq)

    n_tiles = 2 * n // MOE_TM + N_EXPERTS
    counts = cnt[0, :N_EXPERTS].astype(I32)
    tiles_per = (counts + MOE_TM - 1) // MOE_TM
    tile_end = jnp.cumsum(tiles_per)
    row_start = (tile_end - tiles_per) * MOE_TM
    n_used = tile_end[-1]
    idx = meta[:, 0:2].astype(I32)
    pos = meta[:, 4:6].astype(I32)
    dest = row_start[idx] + pos
    tile_ids = jnp.arange(n_tiles, dtype=I32)
    tile_expert = jnp.searchsorted(tile_end, jnp.minimum(tile_ids, n_used - 1), side="right").astype(I32)
    tok = jnp.arange(n, dtype=I32)
    src_tok = jnp.zeros((n_tiles * MOE_TM,), I32).at[dest[:, 0]].set(tok).at[dest[:, 1]].set(tok)
    dest_tiles = dest.reshape(n // COMB_TM, COMB_TM, 2).transpose(0, 2, 1).reshape(-1)

    y_sorted = _moe(tile_expert, n_used.reshape(1).astype(I32), src_tok, h, wg, wu, wd)
    return _combine(dest_tiles, y_sorted, x2, meta, mod4, seq)


def _t5_bucket(dist):
    max_exact = N_BUCKETS // 2
    d = np.maximum(dist, 1).astype(np.float32)
    large = max_exact + (np.log(d / max_exact) / np.log(MAX_DIST / max_exact)
                         * (N_BUCKETS - max_exact)).astype(np.int32)
    large = np.minimum(large, N_BUCKETS - 1)
    return np.where(dist < max_exact, dist, large).astype(np.int32)


def _pack_w_in(w):
    splits = [M_HEADS * M_QK, M_HEADS * M_QK, M_HEADS * M_V, M_HEADS * M_V, 2 * M_HEADS,
              G_HEADS * G_QK, G_HEADS * G_QK, G_HEADS * G_V, G_HEADS * G_V, G_RANK,
              A_HEADS * A_HD, A_KV * A_HD, A_KV * A_HD]
    offs = np.concatenate([[0], np.cumsum(splits)])
    seg = lambda i: w[:, offs[i]:offs[i + 1]]
    main = jnp.concatenate([seg(i) for i in (0, 1, 2, 3, 5, 6, 7, 8, 10, 11, 12)], axis=1)
    small = jnp.concatenate(
        [seg(4), seg(9), jnp.zeros((w.shape[0], LANES - 2 * M_HEADS - G_RANK), w.dtype)], axis=1)
    return main.astype(BF16), small.astype(BF16)


def kernel(x, c, w_ada, b_ada, g_mix_norm, g_ffn_norm, w_in, b_gates_m, w_conv_m, b_conv_m, g_out_m,
           w_gla_decay, b_gla_decay, g_out_g, g_qnorm, g_knorm, sinks, rel_bias, w_out, w_ffn_gate,
           w_ffn_up, w_ffn_down, w_router, w_moe_gate, w_moe_up, w_moe_down):
    bsz, seq, d = x.shape
    depth = w_ada.shape[0]
    n = bsz * seq
    assert seq % 1024 == 0 and d == 2048 and n % MOE_TM == 0

    mod = _ada(c, w_ada, b_ada)

    jj = np.arange(BLK)[:, None]
    ss = np.arange(2 * BLK)[None, :]
    buckets = _t5_bucket(np.clip(jj + BLK - ss, 0, None))
    bias = jnp.transpose(rel_bias.astype(F32)[buckets], (2, 0, 1))
    ind = jnp.asarray(np.kron(np.eye(LANES // A_HD), np.ones((A_HD, A_HD))) / A_HD, BF16)

    x2 = x.reshape(n, d)
    for l in range(depth):
        mod4 = mod[l].reshape(bsz, 6, 1, d)
        w_main, w_small = _pack_w_in(w_in[l])
        pm, ps = _inproj(x2, g_mix_norm[l].reshape(1, d), mod4, w_main, w_small, seq)

        gb = jnp.zeros((1, LANES), F32).at[0, :2 * M_HEADS].set(b_gates_m[l].reshape(-1))
        ym = _mlstm(pm, ps, w_conv_m[l], b_conv_m[l].reshape(1, -1), gb, g_out_m[l].reshape(1, -1),
                    bsz, seq)
        wdec = jnp.zeros((LANES, G_HEADS * G_QK), F32).at[S_GA:S_GA + G_RANK].set(w_gla_decay[l])
        yg = _gla(pm, ps, wdec, b_gla_decay[l].reshape(1, -1), g_out_g[l].reshape(1, -1), bsz, seq)
        gq = jnp.tile(g_qnorm[l], A_HEADS).reshape(1, -1) * (A_HD ** -0.5)
        gk = jnp.tile(g_knorm[l], A_KV).reshape(1, -1)
        ya = _swa(pm, bias, gq, gk, sinks[l], ind, bsz, seq)
        x2 = _outproj(ym, yg, ya, w_out[l].astype(BF16), x2, mod4, seq)

        gf = g_ffn_norm[l].reshape(1, d)
        if l % 2 == 0:
            j = l // 2
            x2 = _ffn(x2, gf, mod4, w_ffn_gate[j].astype(BF16), w_ffn_up[j].astype(BF16),
                      w_ffn_down[j].astype(BF16), seq)
        else:
            j = l // 2
            x2 = _moe_layer(x2, gf, mod4, w_router[j], w_moe_gate[j].astype(BF16),
                            w_moe_up[j].astype(BF16), w_moe_down[j].astype(BF16), seq)
    return x2.reshape(bsz, seq, d)
```

```python
import functools

import numpy as np
import jax
import jax.numpy as jnp
from jax import lax
from jax.experimental import pallas as pl
from jax.experimental.pallas import tpu as pltpu

F32 = jnp.float32
BF16 = jnp.bfloat16
I32 = jnp.int32

M_HEADS = 4
M_QK = 64
M_V = 128
CONV_W = 4
G_HEADS = 4
G_QK = 64
G_V = 128
G_RANK = 16
G_TAU = 16.0
G_CHUNK = 64
G_SUB = 16
A_HD = 64
A_HEADS = 16
A_KV = 2
WINDOW = 128
N_BUCKETS = 32
MAX_DIST = 128
N_EXPERTS = 8
EPS = 1e-6

LANES = 128
BLK = 128
VMEM_LIMIT = 56 * 1024 * 1024

C_MQK, C_MV, C_MO = 0, 512, 1024
C_GQK, C_GV, C_GG = 1536, 2048, 2560
C_AQ, C_AK, C_AV = 3072, 4096, 4224
N_MAIN = 4352
S_GA = 8


def _cparams(sem):
    return pltpu.CompilerParams(dimension_semantics=sem, vmem_limit_bytes=VMEM_LIMIT)


def _dot(a, b):
    return jnp.dot(a, b, preferred_element_type=F32)


def _dot_nt(a, b):
    return lax.dot_general(a, b, (((1,), (1,)), ((), ())), preferred_element_type=F32)


def _split_dot(a, b_f32):
    hi = b_f32.astype(BF16)
    lo = (b_f32 - hi.astype(F32)).astype(BF16)
    return _dot(a, hi) + _dot(a, lo)


def _sigmoid(x):
    return 1.0 / (1.0 + jnp.exp(-x))


def _log_sigmoid(x):
    return jnp.minimum(x, 0.0) - jnp.log(1.0 + jnp.exp(-jnp.abs(x)))


def _norm_mod(x, g, sc, sh):
    ms = jnp.mean(x * x, axis=-1, keepdims=True)
    return (x * lax.rsqrt(ms + EPS) * g) * (1.0 + sc) + sh


NORM_ROWS = 16


def _norm_mod_rows(x_ref, g_ref, sc_ref, sh_ref, out_refs):
    g, sc, sh = g_ref[...], sc_ref[...], sh_ref[...]

    def body(r, carry):
        rows = pl.ds(pl.multiple_of(r * NORM_ROWS, NORM_ROWS), NORM_ROWS)
        h = _norm_mod(x_ref[rows, :], g, sc, sh)
        for ref in out_refs:
            ref[rows, :] = h.astype(ref.dtype)
        return carry

    lax.fori_loop(0, x_ref.shape[0] // NORM_ROWS, body, 0, unroll=8)


def _ada_kernel(c_ref, w_ref, b_ref, o_ref):
    c = c_ref[...]
    cond = (c * _sigmoid(c)).astype(BF16)
    o_ref[...] = _dot(cond, w_ref[...].astype(BF16)) + b_ref[...]


def _ada(c, w_ada, b_ada):
    depth, d, n6 = w_ada.shape
    bsz = c.shape[0]
    tn = 1024
    return pl.pallas_call(
        _ada_kernel,
        grid=(depth, n6 // tn),
        in_specs=[
            pl.BlockSpec((bsz, d), lambda l, n: (0, 0)),
            pl.BlockSpec((None, d, tn), lambda l, n: (l, 0, n)),
            pl.BlockSpec((None, 1, tn), lambda l, n: (l, 0, n)),
        ],
        out_specs=pl.BlockSpec((None, bsz, tn), lambda l, n: (l, 0, n)),
        out_shape=jax.ShapeDtypeStruct((depth, bsz, n6), F32),
        compiler_params=_cparams(("arbitrary", "arbitrary")),
        name="ada_mod",
    )(c, w_ada, b_ada.reshape(depth, 1, n6))


def _inproj_kernel(x_ref, g_ref, sc_ref, sh_ref, wm_ref, ws_ref, om_ref, os_ref, h_ref):
    @pl.when(pl.program_id(1) == 0)
    def _():
        _norm_mod_rows(x_ref, g_ref, sc_ref, sh_ref, [h_ref])
        os_ref[...] = _dot(h_ref[...], ws_ref[...])

    om_ref[...] = _dot(h_ref[...], wm_ref[...]).astype(BF16)


def _inproj(x2, g, mod4, w_main, w_small, seq):
    n, d = x2.shape
    tm, tn = 1024, N_MAIN // 2
    per_b = seq // tm
    mod_spec = lambda k: pl.BlockSpec((None, None, 1, d), lambda i, j: (i // per_b, k, 0, 0))
    return pl.pallas_call(
        _inproj_kernel,
        grid=(n // tm, N_MAIN // tn),
        in_specs=[
            pl.BlockSpec((tm, d), lambda i, j: (i, 0)),
            pl.BlockSpec((1, d), lambda i, j: (0, 0)),
            mod_spec(1), mod_spec(0),
            pl.BlockSpec((d, tn), lambda i, j: (0, j)),
            pl.BlockSpec((d, LANES), lambda i, j: (0, 0)),
        ],
        out_specs=[
            pl.BlockSpec((tm, tn), lambda i, j: (i, j)),
            pl.BlockSpec((tm, LANES), lambda i, j: (i, 0)),
        ],
        out_shape=[jax.ShapeDtypeStruct((n, N_MAIN), BF16),
                   jax.ShapeDtypeStruct((n, LANES), F32)],
        scratch_shapes=[pltpu.VMEM((tm, d), BF16)],
        compiler_params=_cparams(("arbitrary", "arbitrary")),
        name="in_proj",
    )(x2, g, mod4, mod4, w_main, w_small)


def _mlstm_kernel(qk_ref, v_ref, o_ref, sm_ref, wc_ref, bc_ref, gb_ref, go_ref, y_ref,
                  tail_ref, ct_ref, n_ref, m_ref):
    L = BLK
    nqk = M_HEADS * M_QK

    @pl.when(pl.program_id(1) == 0)
    def _():
        tail_ref[...] = jnp.zeros_like(tail_ref)
        ct_ref[...] = jnp.zeros_like(ct_ref)
        n_ref[...] = jnp.zeros_like(n_ref)
        m_ref[...] = jnp.zeros_like(m_ref)

    x = qk_ref[...].astype(F32)
    prev = tail_ref[...]
    row8 = lax.broadcasted_iota(I32, prev.shape, 0)
    acc = x * wc_ref[CONV_W - 1:CONV_W, :] + bc_ref[...]
    for dlt in range(1, CONV_W):
        xs = pltpu.roll(x, dlt, axis=0)
        ps = pltpu.roll(prev, dlt, axis=0)
        top = jnp.where(row8 < dlt, ps, xs[0:8])
        xs = jnp.concatenate([top, xs[8:]], axis=0)
        acc = acc + xs * wc_ref[CONV_W - 1 - dlt:CONV_W - dlt, :]
    tail_ref[...] = x[L - 8:L]
    qk = acc * _sigmoid(acc)
    q = qk[:, :nqk]
    k = qk[:, nqk:] * (M_QK ** -0.5)
    kt_b = k.T.astype(BF16)

    gates = sm_ref[...] + gb_ref[...]
    row = lax.broadcasted_iota(I32, (L, L), 0)
    col = lax.broadcasted_iota(I32, (L, L), 1)
    causal = row >= col
    tri = causal.astype(BF16)
    bcum = _split_dot(tri, _log_sigmoid(gates))
    z = jnp.where(col < M_HEADS, gates, bcum)
    zt = z.T
    lane_q = lax.broadcasted_iota(I32, (L, nqk), 1)

    outs = []
    for h in range(M_HEADS):
        in_head = (lane_q >= h * M_QK) & (lane_q < (h + 1) * M_QK)
        qm = jnp.where(in_head, q, 0.0)
        qm_b = qm.astype(BF16)
        v_h = v_ref[:, h * M_V:(h + 1) * M_V]
        m_prev = m_ref[h:h + 1, 0:1]
        n_h = n_ref[h:h + 1, :]
        ct_h = ct_ref[h]
        bcol = z[:, M_HEADS + h:M_HEADS + h + 1]
        icol = z[:, h:h + 1]
        brow = zt[M_HEADS + h:M_HEADS + h + 1, :]
        irow = zt[h:h + 1, :]
        logd = jnp.where(causal, bcol - brow + irow, -jnp.inf)
        log_inter = bcol + m_prev
        m_row = jnp.maximum(log_inter, jnp.max(logd, axis=1, keepdims=True))
        dmat = jnp.exp(logd - m_row)
        w_inter = jnp.exp(log_inter - m_row)
        s = _dot(qm_b, kt_b) * dmat
        num = w_inter * _dot(qm_b, ct_h.astype(BF16)) + _dot(s.astype(BF16), v_h)
        den = (w_inter * jnp.sum(qm * n_h, axis=1, keepdims=True)
               + jnp.sum(s, axis=1, keepdims=True))
        outs.append(num / jnp.maximum(jnp.abs(den), jnp.exp(-m_row)))
        b_last = bcol[L - 1:L, :]
        log_w = b_last - bcol + icol
        m_new = jnp.maximum(b_last + m_prev, jnp.max(log_w, axis=0, keepdims=True))
        w = jnp.exp(log_w - m_new)
        decay = jnp.exp(b_last + m_prev - m_new)
        vw = (v_h.astype(F32) * w).astype(BF16)
        ct_ref[h] = decay * ct_h + _dot(kt_b, vw)
        n_ref[h:h + 1, :] = decay * n_h + jnp.sum(k * w, axis=0, keepdims=True)
        m_ref[h:h + 1, :] = jnp.broadcast_to(m_new, (1, LANES))

    og = _sigmoid(o_ref[...].astype(F32))
    for h in range(M_HEADS):
        hh = outs[h]
        ms = jnp.mean(hh * hh, axis=1, keepdims=True)
        sl = slice(h * M_V, (h + 1) * M_V)
        y = hh * lax.rsqrt(ms + EPS) * go_ref[:, sl]
        y_ref[:, sl] = (y * og[:, sl]).astype(BF16)


def _mlstm(pm, ps, wc, bc, gb, go, bsz, seq):
    nb = seq // BLK
    wd = M_HEADS * M_V
    rows = lambda cb: pl.BlockSpec((BLK, wd), lambda b, c: (b * nb + c, cb))
    full = lambda shp: pl.BlockSpec(shp, lambda b, c: (0,) * len(shp))
    return pl.pallas_call(
        _mlstm_kernel,
        grid=(bsz, nb),
        in_specs=[
            rows(C_MQK // wd), rows(C_MV // wd), rows(C_MO // wd),
            pl.BlockSpec((BLK, LANES), lambda b, c: (b * nb + c, 0)),
            full((CONV_W, 2 * M_HEADS * M_QK)), full((1, 2 * M_HEADS * M_QK)),
            full((1, LANES)), full((1, wd)),
        ],
        out_specs=pl.BlockSpec((BLK, wd), lambda b, c: (b * nb + c, 0)),
        out_shape=jax.ShapeDtypeStruct((bsz * seq, wd), BF16),
        scratch_shapes=[
            pltpu.VMEM((8, 2 * M_HEADS * M_QK), F32),
            pltpu.VMEM((M_HEADS, M_HEADS * M_QK, M_V), F32),
            pltpu.VMEM((8, M_HEADS * M_QK), F32),
            pltpu.VMEM((8, LANES), F32),
        ],
        compiler_params=_cparams(("arbitrary", "arbitrary")),
        name="mlstm",
    )(pm, pm, pm, ps, wc, bc, gb, go)


def _gla_kernel(qk_ref, v_ref, g_ref, sm_ref, wd_ref, bd_ref, go_ref, y_ref, s_ref, lhs_ref):
    L, C, SC = BLK, G_CHUNK, G_SUB
    nqk = G_HEADS * G_QK
    pair_k = 2 * G_QK
    pair_v = 2 * G_V
    n_pairs = G_HEADS // 2

    @pl.when(pl.program_id(1) == 0)
    def _():
        s_ref[...] = jnp.zeros_like(s_ref)

    zdec = _dot(sm_ref[...].astype(BF16), wd_ref[...].astype(BF16)) + bd_ref[...]
    log_a = _log_sigmoid(zdec) / G_TAU
    row = lax.broadcasted_iota(I32, (L, L), 0)
    col = lax.broadcasted_iota(I32, (L, L), 1)
    same_chunk = (row // C) == (col // C)
    tri = ((row >= col) & same_chunk).astype(BF16)
    b = _split_dot(tri, log_a)
    bt = b.T

    q = qk_ref[:, :nqk].astype(F32) * (G_QK ** -0.5)
    k = qk_ref[:, nqk:].astype(F32)
    b_end = jnp.concatenate(
        [jnp.broadcast_to(b[(cc + 1) * C - 1:(cc + 1) * C, :], (C, nqk)) for cc in range(L // C)], axis=0)
    kdt = (k * jnp.exp(b_end - b)).T
    qe = q * jnp.exp(b)

    dloc = col - (row // SC) * SC
    diag_sel = jnp.where((dloc >= 0) & (dloc <= row % SC), dloc, -1)
    sub_r = (row % C) // SC
    off_sel = jnp.where(same_chunk & ((col % C) // SC < sub_r), sub_r, -1)
    diag_sel2 = jnp.concatenate([diag_sel, diag_sel], axis=1)
    off_sel2 = jnp.concatenate([off_sel, off_sel], axis=1)

    lane_s = lax.broadcasted_iota(I32, (pair_k, L), 1)
    head_k = lax.broadcasted_iota(I32, (L, pair_k), 1) // G_QK
    rblk = lax.broadcasted_iota(I32, (pair_k, pair_v), 0) // G_QK
    cblk = lax.broadcasted_iota(I32, (pair_k, pair_v), 1) // G_V
    blockdiag = rblk == cblk
    vrow = lax.broadcasted_iota(I32, (2 * L, pair_v), 0) // L
    vcol = lax.broadcasted_iota(I32, (2 * L, pair_v), 1) // G_V
    vdiag = vrow == vcol

    def stack_heads(x):
        return jnp.concatenate([jnp.where(head_k == hh, x, 0.0) for hh in range(2)], axis=0).astype(BF16)

    pair_out = []
    for p in range(n_pairs):
        kl = slice(p * pair_k, (p + 1) * pair_k)
        vl = slice(p * pair_v, (p + 1) * pair_v)
        bp, qp, kp = b[:, kl], q[:, kl], k[:, kl]
        v_p = v_ref[:, vl]
        b3 = bp.reshape(L // SC, SC, pair_k)

        def sub_row(i, b3=b3):
            return jnp.broadcast_to(b3[:, i:i + 1, :], b3.shape).reshape(L, pair_k)

        for i in range(SC):
            e = jnp.exp(jnp.minimum(bp - sub_row(i), 0.0))
            lhs_ref[i * L:(i + 1) * L, :] = (qp * e).astype(BF16)
        r = _dot_nt(lhs_ref[...], stack_heads(kp))
        att = jnp.zeros((L, 2 * L), F32)
        for i in range(SC):
            att = att + jnp.where(diag_sel2 == i, r[i * L:(i + 1) * L], 0.0)

        qs = (qp * jnp.exp(bp - sub_row(0))).astype(BF16)
        for sub in range(1, C // SC):
            bref = jnp.concatenate(
                [jnp.broadcast_to(bp[cc * C + sub * SC:cc * C + sub * SC + 1, :], (C, pair_k))
                 for cc in range(L // C)], axis=0)
            ks = kp * jnp.exp(jnp.minimum(bref - bp, 0.0))
            att = att + jnp.where(off_sel2 == sub, _dot_nt(qs, stack_heads(ks)), 0.0)

        vblk = jnp.where(vdiag, jnp.concatenate([v_p, v_p], axis=0), 0)
        intra = _dot(att.astype(BF16), vblk)

        outs = []
        for cc in range(L // C):
            r0 = cc * C
            s_p = s_ref[p]
            outs.append(_dot(qe[r0:r0 + C, kl].astype(BF16), s_p.astype(BF16)) + intra[r0:r0 + C])
            in_chunk = (lane_s >= r0) & (lane_s < r0 + C)
            kd = jnp.where(in_chunk, kdt[kl, :], 0.0).astype(BF16)
            upd = jnp.where(blockdiag, _dot(kd, v_p), 0.0)
            dcol = jnp.exp(bt[kl, r0 + C - 1:r0 + C])
            s_ref[p] = dcol * s_p + upd
        pair_out.append(outs)

    gate = g_ref[...].astype(F32)
    gate = gate * _sigmoid(gate)
    for p in range(n_pairs):
        o_p = jnp.concatenate(pair_out[p], axis=0)
        for hh in range(2):
            h = 2 * p + hh
            oh = o_p[:, hh * G_V:(hh + 1) * G_V]
            ms = jnp.mean(oh * oh, axis=1, keepdims=True)
            sl = slice(h * G_V, (h + 1) * G_V)
            y = oh * lax.rsqrt(ms + EPS) * go_ref[:, sl]
            y_ref[:, sl] = (y * gate[:, sl]).astype(BF16)


def _gla(pm, ps, wd, bd, go, bsz, seq):
    nb = seq // BLK
    w = G_HEADS * G_V
    nqk = G_HEADS * G_QK
    rows = lambda cb: pl.BlockSpec((BLK, w), lambda b, c: (b * nb + c, cb))
    full = lambda shp: pl.BlockSpec(shp, lambda b, c: (0,) * len(shp))
    return pl.pallas_call(
        _gla_kernel,
        grid=(bsz, nb),
        in_specs=[
            rows(C_GQK // w), rows(C_GV // w), rows(C_GG // w),
            pl.BlockSpec((BLK, LANES), lambda b, c: (b * nb + c, 0)),
            full((LANES, nqk)), full((1, nqk)), full((1, w)),
        ],
        out_specs=pl.BlockSpec((BLK, w), lambda b, c: (b * nb + c, 0)),
        out_shape=jax.ShapeDtypeStruct((bsz * seq, w), BF16),
        scratch_shapes=[
            pltpu.VMEM((G_HEADS // 2, 2 * G_QK, 2 * G_V), F32),
            pltpu.VMEM((G_SUB * BLK, 2 * G_QK), BF16),
        ],
        compiler_params=_cparams(("arbitrary", "arbitrary")),
        name="gla",
    )(pm, pm, pm, ps, wd, bd, go)


def _group_mean_sq(x, ind):
    return _split_dot_rhs(x * x, ind)


def _split_dot_rhs(a_f32, b_bf16):
    hi = a_f32.astype(BF16)
    lo = (a_f32 - hi.astype(F32)).astype(BF16)
    return _dot(hi, b_bf16) + _dot(lo, b_bf16)


def _swa_kernel(q_ref, kc_ref, kp_ref, vc_ref, vp_ref, bias_ref, gq_ref, gk_ref, sink_ref, ind_ref,
                y_ref):
    T = BLK
    heads_per_tile = LANES // A_HD
    n_tiles = A_HEADS // heads_per_tile
    grp = A_HEADS // A_KV
    first_key = jnp.where(pl.program_id(1) == 0, T, 0)
    ind = ind_ref[...]

    qn = []
    for t in range(n_tiles):
        sl = slice(t * LANES, (t + 1) * LANES)
        qt = q_ref[:, sl].astype(F32)
        ms = _group_mean_sq(qt, ind)
        qn.append((qt * lax.rsqrt(ms + EPS) * gq_ref[:, sl]).astype(BF16))
    kk = jnp.concatenate([kp_ref[...], kc_ref[...]], axis=0).astype(F32)
    kn = kk * lax.rsqrt(_group_mean_sq(kk, ind) + EPS) * gk_ref[...]
    vv = jnp.concatenate([vp_ref[...], vc_ref[...]], axis=0).astype(F32)
    k_sw = pltpu.roll(kn, A_HD, axis=1)
    v_sw = pltpu.roll(vv, A_HD, axis=1)
    half = lax.broadcasted_iota(I32, (2 * T, LANES), 1) // A_HD

    j = lax.broadcasted_iota(I32, (T, 2 * T), 0)
    s = lax.broadcasted_iota(I32, (T, 2 * T), 1)
    dist = j + T - s
    mask = (dist >= 0) & (dist < WINDOW) & (s >= first_key)

    acc = [None] * n_tiles
    for g in range(A_KV):
        for p in range(heads_per_tile):
            ksrc, vsrc = (kn, vv) if g == p else (k_sw, v_sw)
            kz = jnp.where(half == p, ksrc, 0.0).astype(BF16)
            vz = jnp.where(half == p, vsrc, 0.0).astype(BF16)
            heads = [g * grp + heads_per_tile * u + p for u in range(grp // heads_per_tile)]
            qs = jnp.concatenate([qn[h // heads_per_tile] for h in heads], axis=0)
            sc_all = _dot_nt(qs, kz)
            probs, dens = [], []
            for u, h in enumerate(heads):
                sc = jnp.where(mask, sc_all[u * T:(u + 1) * T] + bias_ref[h], -jnp.inf)
                sink = sink_ref[h]
                m = jnp.maximum(jnp.max(sc, axis=1, keepdims=True), sink)
                pr = jnp.exp(sc - m)
                dens.append(jnp.sum(pr, axis=1, keepdims=True) + jnp.exp(sink - m))
                probs.append(pr.astype(BF16))
            o_all = _dot(jnp.concatenate(probs, axis=0), vz)
            for u, h in enumerate(heads):
                o = o_all[u * T:(u + 1) * T] / dens[u]
                t = h // heads_per_tile
                acc[t] = o if acc[t] is None else acc[t] + o
    for t in range(n_tiles):
        y_ref[:, t * LANES:(t + 1) * LANES] = acc[t].astype(BF16)


def _swa(pm, bias, gq, gk, sinks, ind, bsz, seq):
    nb = seq // BLK
    wq = A_HEADS * A_HD
    cur = lambda cb: pl.BlockSpec((BLK, LANES), lambda b, c: (b * nb + c, cb))
    prv = lambda cb: pl.BlockSpec((BLK, LANES), lambda b, c: (b * nb + jnp.maximum(c - 1, 0), cb))
    full = lambda shp: pl.BlockSpec(shp, lambda b, c: (0,) * len(shp))
    return pl.pallas_call(
        _swa_kernel,
        grid=(bsz, nb),
        in_specs=[
            pl.BlockSpec((BLK, wq), lambda b, c: (b * nb + c, C_AQ // wq)),
            cur(C_AK // LANES), prv(C_AK // LANES), cur(C_AV // LANES), prv(C_AV // LANES),
            full((A_HEADS, BLK, 2 * BLK)), full((1, wq)), full((1, LANES)),
            pl.BlockSpec(memory_space=pltpu.SMEM),
            full((LANES, LANES)),
        ],
        out_specs=pl.BlockSpec((BLK, wq), lambda b, c: (b * nb + c, 0)),
        out_shape=jax.ShapeDtypeStruct((bsz * seq, wq), BF16),
        compiler_params=_cparams(("arbitrary", "arbitrary")),
        name="swa",
    )(pm, pm, pm, pm, pm, bias, gq, gk, sinks, ind)


def _outproj_kernel(ym_ref, yg_ref, ya_ref, w_ref, x_ref, gt_ref, o_ref):
    y = jnp.concatenate([ym_ref[...], yg_ref[...], ya_ref[...]], axis=1)
    o_ref[...] = x_ref[...] + gt_ref[...] * _dot(y, w_ref[...])


def _outproj(ym, yg, ya, w, x2, mod4, seq):
    n, d = x2.shape
    tm, tn = 1024, d
    per_b = seq // tm
    return pl.pallas_call(
        _outproj_kernel,
        grid=(n // tm, d // tn),
        in_specs=[
            pl.BlockSpec((tm, ym.shape[1]), lambda i, j: (i, 0)),
            pl.BlockSpec((tm, yg.shape[1]), lambda i, j: (i, 0)),
            pl.BlockSpec((tm, ya.shape[1]), lambda i, j: (i, 0)),
            pl.BlockSpec((d, tn), lambda i, j: (0, j), pipeline_mode=pl.Buffered(1)),
            pl.BlockSpec((tm, tn), lambda i, j: (i, j)),
            pl.BlockSpec((None, None, 1, tn), lambda i, j: (i // per_b, 2, 0, j)),
        ],
        out_specs=pl.BlockSpec((tm, tn), lambda i, j: (i, j)),
        out_shape=jax.ShapeDtypeStruct((n, d), F32),
        compiler_params=_cparams(("arbitrary", "arbitrary")),
        name="out_proj",
    )(ym, yg, ya, w, x2, mod4)


def _swiglu_tile(h, wg_ref, wu_ref, wd_ref):
    gate = _dot(h, wg_ref[...].astype(BF16))
    up = _dot(h, wu_ref[...].astype(BF16))
    a = (gate * _sigmoid(gate) * up).astype(BF16)
    return _dot(a, wd_ref[...].astype(BF16))


def _ffn_kernel(x_ref, g_ref, sc_ref, sh_ref, gt_ref, wg_ref, wu_ref, wd_ref, o_ref, h_ref):
    f = pl.program_id(1)

    @pl.when(f == 0)
    def _():
        _norm_mod_rows(x_ref, g_ref, sc_ref, sh_ref, [h_ref])
        o_ref[...] = jnp.zeros_like(o_ref)

    o_ref[...] += _swiglu_tile(h_ref[...], wg_ref, wu_ref, wd_ref)

    @pl.when(f == pl.num_programs(1) - 1)
    def _():
        o_ref[...] = x_ref[...] + gt_ref[...] * o_ref[...]


def _ffn(x2, g, mod4, wg, wu, wd, seq):
    n, d = x2.shape
    dff = wg.shape[1]
    tm, tf = 1024, 512
    per_b = seq // tm
    mod_spec = lambda k: pl.BlockSpec((None, None, 1, d), lambda i, f: (i // per_b, k, 0, 0))
    return pl.pallas_call(
        _ffn_kernel,
        grid=(n // tm, dff // tf),
        in_specs=[
            pl.BlockSpec((tm, d), lambda i, f: (i, 0), pipeline_mode=pl.Buffered(1)),
            pl.BlockSpec((1, d), lambda i, f: (0, 0)),
            mod_spec(4), mod_spec(3), mod_spec(5),
            pl.BlockSpec((d, tf), lambda i, f: (0, f)),
            pl.BlockSpec((d, tf), lambda i, f: (0, f)),
            pl.BlockSpec((tf, d), lambda i, f: (f, 0)),
        ],
        out_specs=pl.BlockSpec((tm, d), lambda i, f: (i, 0), pipeline_mode=pl.Buffered(1)),
        out_shape=jax.ShapeDtypeStruct((n, d), F32),
        scratch_shapes=[pltpu.VMEM((tm, d), BF16)],
        compiler_params=_cparams(("arbitrary", "arbitrary")),
        name="ffn_dense",
    )(x2, g, mod4, mod4, mod4, wg, wu, wd)


def _router_kernel(x_ref, g_ref, sc_ref, sh_ref, wh_ref, wl_ref, tri_ref, h_ref, meta_ref, cnt_ref,
                   run_ref):
    @pl.when(pl.program_id(0) == 0)
    def _():
        run_ref[...] = jnp.zeros_like(run_ref)

    _norm_mod_rows(x_ref, g_ref, sc_ref, sh_ref, [h_ref])
    h = h_ref[...]
    h_hi = h.astype(BF16)
    h_lo = (h - h_hi.astype(F32)).astype(BF16)
    logits = _dot(h_hi, wh_ref[...]) + _dot(h_lo, wh_ref[...]) + _dot(h_hi, wl_ref[...])
    lane = lax.broadcasted_iota(I32, logits.shape, 1).astype(F32)
    l1 = jnp.where(lane < N_EXPERTS, logits, -jnp.inf)
    m1 = jnp.max(l1, axis=1, keepdims=True)
    i1 = jnp.min(jnp.where(l1 == m1, lane, float(LANES)), axis=1, keepdims=True)
    l2 = jnp.where(lane == i1, -jnp.inf, l1)
    m2 = jnp.max(l2, axis=1, keepdims=True)
    i2 = jnp.min(jnp.where(l2 == m2, lane, float(LANES)), axis=1, keepdims=True)
    e2 = jnp.exp(m2 - m1)
    w1 = 1.0 / (1.0 + e2)
    w2 = e2 / (1.0 + e2)
    member = jnp.where((lane == i1) | (lane == i2), 1.0, 0.0)
    run = run_ref[0:1, :]
    rank = _dot(tri_ref[...], member.astype(BF16)) + run
    p1 = jnp.sum(jnp.where(lane == i1, rank, 0.0), axis=1, keepdims=True)
    p2 = jnp.sum(jnp.where(lane == i2, rank, 0.0), axis=1, keepdims=True)
    run = run + jnp.sum(member, axis=0, keepdims=True)
    run_ref[0:1, :] = run
    cnt_ref[...] = jnp.broadcast_to(run, cnt_ref.shape)
    meta = jnp.zeros(logits.shape, F32)
    for idx, val in enumerate([i1, i2, w1, w2, p1, p2]):
        meta = jnp.where(lane == float(idx), val, meta)
    meta_ref[...] = meta


def _router(x2, g, mod4, wr_hi, wr_lo, seq):
    n, d = x2.shape
    tm = 512
    per_b = seq // tm
    tri = jnp.asarray(np.tril(np.ones((tm, tm), np.float32), -1), BF16)
    mod_spec = lambda k: pl.BlockSpec((None, None, 1, d), lambda i: (i // per_b, k, 0, 0))
    return pl.pallas_call(
        _router_kernel,
        grid=(n // tm,),
        in_specs=[
            pl.BlockSpec((tm, d), lambda i: (i, 0)),
            pl.BlockSpec((1, d), lambda i: (0, 0)),
            mod_spec(4), mod_spec(3),
            pl.BlockSpec((d, LANES), lambda i: (0, 0)),
            pl.BlockSpec((d, LANES), lambda i: (0, 0)),
            pl.BlockSpec((tm, tm), lambda i: (0, 0)),
        ],
        out_specs=[
            pl.BlockSpec((tm, d), lambda i: (i, 0)),
            pl.BlockSpec((tm, LANES), lambda i: (i, 0)),
            pl.BlockSpec((8, LANES), lambda i: (0, 0)),
        ],
        out_shape=[jax.ShapeDtypeStruct((n, d), F32),
                   jax.ShapeDtypeStruct((n, LANES), F32),
                   jax.ShapeDtypeStruct((8, LANES), F32)],
        scratch_shapes=[pltpu.VMEM((8, LANES), F32)],
        compiler_params=_cparams(("arbitrary",)),
        name="router",
    )(x2, g, mod4, mod4, wr_hi, wr_lo, tri)


MOE_TM = 1024


def _gather_rows(idx_ref, src_hbm, dst_ref, sem, n_rows, dst_off=0):
    def row_copy(r):
        return pltpu.make_async_copy(src_hbm.at[pl.ds(idx_ref[dst_off + r], 1), :],
                                     dst_ref.at[pl.ds(r, 1), :], sem)

    def start(r, c):
        row_copy(r).start()
        return c

    def wait(r, c):
        row_copy(r).wait()
        return c

    lax.fori_loop(0, n_rows, start, 0, unroll=8)
    lax.fori_loop(0, n_rows, wait, 0, unroll=8)


def _moe_kernel(te_ref, nu_ref, tok_hbm, h_hbm, wg_ref, wu_ref, wd_ref, o_ref,
                idx_ref, xf_ref, xb_ref, sem_i, sem_r):
    t = pl.program_id(0)
    f = pl.program_id(1)
    used = t < nu_ref[0]

    @pl.when(used & (f == 0))
    def _():
        cp = pltpu.make_async_copy(tok_hbm.at[pl.ds(pl.multiple_of(t * MOE_TM, MOE_TM), MOE_TM)],
                                   idx_ref, sem_i)
        cp.start()
        cp.wait()
        half = MOE_TM // 2
        for part in range(2):
            _gather_rows(idx_ref, h_hbm, xf_ref, sem_r, half, part * half)
            xb_ref[part * half:(part + 1) * half, :] = xf_ref[...].astype(BF16)
        o_ref[...] = jnp.zeros_like(o_ref)

    @pl.when(used)
    def _():
        o_ref[...] += _swiglu_tile(xb_ref[...], wg_ref, wu_ref, wd_ref)

    @pl.when(jnp.logical_not(used) & (f == 0))
    def _():
        o_ref[...] = jnp.zeros_like(o_ref)


def _moe(tile_expert, n_used, src_tok, h, wg, wu, wd):
    n, d = h.shape
    n_tiles = tile_expert.shape[0]
    dff = wg.shape[2]
    tf = 512
    nf = dff // tf

    def f_blk(t, f, nu):
        return jnp.where(t < nu[0], f, nf - 1)

    grid_spec = pltpu.PrefetchScalarGridSpec(
        num_scalar_prefetch=2,
        grid=(n_tiles, nf),
        in_specs=[
            pl.BlockSpec(memory_space=pl.ANY),
            pl.BlockSpec(memory_space=pl.ANY),
            pl.BlockSpec((None, d, tf), lambda t, f, te, nu: (te[t], 0, f_blk(t, f, nu))),
            pl.BlockSpec((None, d, tf), lambda t, f, te, nu: (te[t], 0, f_blk(t, f, nu))),
            pl.BlockSpec((None, tf, d), lambda t, f, te, nu: (te[t], f_blk(t, f, nu), 0)),
        ],
        out_specs=pl.BlockSpec((MOE_TM, d), lambda t, f, te, nu: (t, 0), pipeline_mode=pl.Buffered(1)),
        scratch_shapes=[
            pltpu.SMEM((MOE_TM,), I32),
            pltpu.VMEM((MOE_TM // 2, d), F32),
            pltpu.VMEM((MOE_TM, d), BF16),
            pltpu.SemaphoreType.DMA(()),
            pltpu.SemaphoreType.DMA(()),
        ],
    )
    return pl.pallas_call(
        _moe_kernel,
        grid_spec=grid_spec,
        out_shape=jax.ShapeDtypeStruct((n_tiles * MOE_TM, d), F32),
        compiler_params=_cparams(("arbitrary", "arbitrary")),
        name="moe_experts",
    )(tile_expert, n_used, src_tok, h, wg, wu, wd)


COMB_TM = 512


def _combine_kernel(dest_hbm, y_hbm, x_ref, meta_ref, gt_ref, o_ref, idx_ref, y0_ref, y1_ref, sem_i, sem_r):
    i = pl.program_id(0)
    cp = pltpu.make_async_copy(
        dest_hbm.at[pl.ds(pl.multiple_of(i * 2 * COMB_TM, 2 * COMB_TM), 2 * COMB_TM)], idx_ref, sem_i)
    cp.start()
    cp.wait()
    _gather_rows(idx_ref, y_hbm, y0_ref, sem_r, COMB_TM, 0)
    _gather_rows(idx_ref, y_hbm, y1_ref, sem_r, COMB_TM, COMB_TM)
    w1 = meta_ref[:, 2:3]
    w2 = meta_ref[:, 3:4]
    o_ref[...] = x_ref[...] + gt_ref[...] * (w1 * y0_ref[...] + w2 * y1_ref[...])


def _combine(dest, y_sorted, x2, meta, mod4, seq):
    n, d = x2.shape
    tm = COMB_TM
    per_b = seq // tm
    return pl.pallas_call(
        _combine_kernel,
        grid=(n // tm,),
        in_specs=[
            pl.BlockSpec(memory_space=pl.ANY),
            pl.BlockSpec(memory_space=pl.ANY),
            pl.BlockSpec((tm, d), lambda i: (i, 0)),
            pl.BlockSpec((tm, LANES), lambda i: (i, 0)),
            pl.BlockSpec((None, None, 1, d), lambda i: (i // per_b, 5, 0, 0)),
        ],
        out_specs=pl.BlockSpec((tm, d), lambda i: (i, 0)),
        out_shape=jax.ShapeDtypeStruct((n, d), F32),
        scratch_shapes=[
            pltpu.SMEM((2 * tm,), I32),
            pltpu.VMEM((tm, d), F32),
            pltpu.VMEM((tm, d), F32),
            pltpu.SemaphoreType.DMA(()),
            pltpu.SemaphoreType.DMA(()),
        ],
        compiler_params=_cparams(("arbitrary",)),
        name="moe_combine",
    )(dest, y_sorted, x2, meta, mod4)


def _moe_layer(x2, g, mod4, w_router, wg, wu, wd, seq):
    n, d = x2.shape
    wr = jnp.zeros((d, LANES), F32).at[:, :N_EXPERTS].set(w_router)
    wr_hi = wr.astype(BF16)
    wr_lo = (wr - wr_hi.astype(F32)).astype(BF16)
    h, meta, cnt = _router(x2, g, mod4, wr_hi, wr_lo, seq)

    n_tiles = 2 * n // MOE_TM + N_EXPERTS
    counts = cnt[0, :N_EXPERTS].astype(I32)
    tiles_per = (counts + MOE_TM - 1) // MOE_TM
    tile_end = jnp.cumsum(tiles_per)
    row_start = (tile_end - tiles_per) * MOE_TM
    n_used = tile_end[-1]
    idx = meta[:, 0:2].astype(I32)
    pos = meta[:, 4:6].astype(I32)
    dest = row_start[idx] + pos
    tile_ids = jnp.arange(n_tiles, dtype=I32)
    tile_expert = jnp.searchsorted(tile_end, jnp.minimum(tile_ids, n_used - 1), side="right").astype(I32)
    tok = jnp.arange(n, dtype=I32)
    src_tok = jnp.zeros((n_tiles * MOE_TM,), I32).at[dest[:, 0]].set(tok).at[dest[:, 1]].set(tok)
    dest_tiles = dest.reshape(n // COMB_TM, COMB_TM, 2).transpose(0, 2, 1).reshape(-1)

    y_sorted = _moe(tile_expert, n_used.reshape(1).astype(I32), src_tok, h, wg, wu, wd)
    return _combine(dest_tiles, y_sorted, x2, meta, mod4, seq)


def _t5_bucket(dist):
    max_exact = N_BUCKETS // 2
    d = np.maximum(dist, 1).astype(np.float32)
    large = max_exact + (np.log(d / max_exact) / np.log(MAX_DIST / max_exact)
                         * (N_BUCKETS - max_exact)).astype(np.int32)
    large = np.minimum(large, N_BUCKETS - 1)
    return np.where(dist < max_exact, dist, large).astype(np.int32)


def _pack_w_in(w):
    splits = [M_HEADS * M_QK, M_HEADS * M_QK, M_HEADS * M_V, M_HEADS * M_V, 2 * M_HEADS,
              G_HEADS * G_QK, G_HEADS * G_QK, G_HEADS * G_V, G_HEADS * G_V, G_RANK,
              A_HEADS * A_HD, A_KV * A_HD, A_KV * A_HD]
    offs = np.concatenate([[0], np.cumsum(splits)])
    seg = lambda i: w[:, offs[i]:offs[i + 1]]
    main = jnp.concatenate([seg(i) for i in (0, 1, 2, 3, 5, 6, 7, 8, 10, 11, 12)], axis=1)
    small = jnp.concatenate(
        [seg(4), seg(9), jnp.zeros((w.shape[0], LANES - 2 * M_HEADS - G_RANK), w.dtype)], axis=1)
    return main.astype(BF16), small.astype(BF16)


def kernel(x, c, w_ada, b_ada, g_mix_norm, g_ffn_norm, w_in, b_gates_m, w_conv_m, b_conv_m, g_out_m,
           w_gla_decay, b_gla_decay, g_out_g, g_qnorm, g_knorm, sinks, rel_bias, w_out, w_ffn_gate,
           w_ffn_up, w_ffn_down, w_router, w_moe_gate, w_moe_up, w_moe_down):
    bsz, seq, d = x.shape
    depth = w_ada.shape[0]
    n = bsz * seq
    assert seq % 1024 == 0 and d == 2048 and n % MOE_TM == 0

    mod = _ada(c, w_ada, b_ada)

    jj = np.arange(BLK)[:, None]
    ss = np.arange(2 * BLK)[None, :]
    buckets = _t5_bucket(np.clip(jj + BLK - ss, 0, None))
    bias = jnp.transpose(rel_bias.astype(F32)[buckets], (2, 0, 1))
    ind = jnp.asarray(np.kron(np.eye(LANES // A_HD), np.ones((A_HD, A_HD))) / A_HD, BF16)

    x2 = x.reshape(n, d)
    for l in range(depth):
        mod4 = mod[l].reshape(bsz, 6, 1, d)
        w_main, w_small = _pack_w_in(w_in[l])
        pm, ps = _inproj(x2, g_mix_norm[l].reshape(1, d), mod4, w_main, w_small, seq)

        gb = jnp.zeros((1, LANES), F32).at[0, :2 * M_HEADS].set(b_gates_m[l].reshape(-1))
        ym = _mlstm(pm, ps, w_conv_m[l], b_conv_m[l].reshape(1, -1), gb, g_out_m[l].reshape(1, -1),
                    bsz, seq)
        wdec = jnp.zeros((LANES, G_HEADS * G_QK), F32).at[S_GA:S_GA + G_RANK].set(w_gla_decay[l])
        yg = _gla(pm, ps, wdec, b_gla_decay[l].reshape(1, -1), g_out_g[l].reshape(1, -1), bsz, seq)
        gq = jnp.tile(g_qnorm[l], A_HEADS).reshape(1, -1) * (A_HD ** -0.5)
        gk = jnp.tile(g_knorm[l], A_KV).reshape(1, -1)
        ya = _swa(pm, bias, gq, gk, sinks[l], ind, bsz, seq)
        x2 = _outproj(ym, yg, ya, w_out[l].astype(BF16), x2, mod4, seq)

        gf = g_ffn_norm[l].reshape(1, d)
        if l % 2 == 0:
            j = l // 2
            x2 = _ffn(x2, gf, mod4, w_ffn_gate[j], w_ffn_up[j], w_ffn_down[j], seq)
        else:
            j = l // 2
            x2 = _moe_layer(x2, gf, mod4, w_router[j], w_moe_gate[j], w_moe_up[j], w_moe_down[j], seq)
    return x2.reshape(bsz, seq, d)
```

```python
import functools

import numpy as np
import jax
import jax.numpy as jnp
from jax import lax
from jax.experimental import pallas as pl
from jax.experimental.pallas import tpu as pltpu

F32 = jnp.float32
BF16 = jnp.bfloat16
I32 = jnp.int32

M_HEADS = 4
M_QK = 64
M_V = 128
CONV_W = 4
G_HEADS = 4
G_QK = 64
G_V = 128
G_RANK = 16
G_TAU = 16.0
G_CHUNK = 64
G_SUB = 16
A_HD = 64
A_HEADS = 16
A_KV = 2
WINDOW = 128
N_BUCKETS = 32
MAX_DIST = 128
N_EXPERTS = 8
EPS = 1e-6

LANES = 128
BLK = 128
VMEM_LIMIT = 56 * 1024 * 1024

C_MQK, C_MV, C_MO = 0, 512, 1024
C_GQK, C_GV, C_GG = 1536, 2048, 2560
C_AQ, C_AK, C_AV = 3072, 4096, 4224
N_MAIN = 4352
S_GA = 8


def _cparams(sem):
    return pltpu.CompilerParams(dimension_semantics=sem, vmem_limit_bytes=VMEM_LIMIT)


def _dot(a, b):
    return jnp.dot(a, b, preferred_element_type=F32)


def _dot_nt(a, b):
    return lax.dot_general(a, b, (((1,), (1,)), ((), ())), preferred_element_type=F32)


def _split_dot(a, b_f32):
    hi = b_f32.astype(BF16)
    lo = (b_f32 - hi.astype(F32)).astype(BF16)
    return _dot(a, hi) + _dot(a, lo)


def _sigmoid(x):
    return 1.0 / (1.0 + jnp.exp(-x))


def _log_sigmoid(x):
    return jnp.minimum(x, 0.0) - jnp.log(1.0 + jnp.exp(-jnp.abs(x)))


def _norm_mod(x, g, sc, sh):
    ms = jnp.mean(x * x, axis=-1, keepdims=True)
    return (x * lax.rsqrt(ms + EPS) * g) * (1.0 + sc) + sh


NORM_ROWS = 16


def _norm_mod_rows(x_ref, g_ref, sc_ref, sh_ref, out_refs):
    g, sc, sh = g_ref[...], sc_ref[...], sh_ref[...]

    def body(r, carry):
        rows = pl.ds(pl.multiple_of(r * NORM_ROWS, NORM_ROWS), NORM_ROWS)
        h = _norm_mod(x_ref[rows, :], g, sc, sh)
        for ref in out_refs:
            ref[rows, :] = h.astype(ref.dtype)
        return carry

    lax.fori_loop(0, x_ref.shape[0] // NORM_ROWS, body, 0, unroll=8)


def _ada_kernel(c_ref, w_ref, b_ref, o_ref):
    c = c_ref[...]
    cond = (c * _sigmoid(c)).astype(BF16)
    o_ref[...] = _dot(cond, w_ref[...].astype(BF16)) + b_ref[...]


def _ada(c, w_ada, b_ada):
    depth, d, n6 = w_ada.shape
    bsz = c.shape[0]
    tn = 1024
    return pl.pallas_call(
        _ada_kernel,
        grid=(depth, n6 // tn),
        in_specs=[
            pl.BlockSpec((bsz, d), lambda l, n: (0, 0)),
            pl.BlockSpec((None, d, tn), lambda l, n: (l, 0, n)),
            pl.BlockSpec((None, 1, tn), lambda l, n: (l, 0, n)),
        ],
        out_specs=pl.BlockSpec((None, bsz, tn), lambda l, n: (l, 0, n)),
        out_shape=jax.ShapeDtypeStruct((depth, bsz, n6), F32),
        compiler_params=_cparams(("arbitrary", "arbitrary")),
        name="ada_mod",
    )(c, w_ada, b_ada.reshape(depth, 1, n6))


def _inproj_kernel(x_ref, g_ref, sc_ref, sh_ref, wm_ref, ws_ref, om_ref, os_ref, h_ref):
    @pl.when(pl.program_id(1) == 0)
    def _():
        _norm_mod_rows(x_ref, g_ref, sc_ref, sh_ref, [h_ref])
        os_ref[...] = _dot(h_ref[...], ws_ref[...])

    om_ref[...] = _dot(h_ref[...], wm_ref[...]).astype(BF16)


def _inproj(x2, g, mod4, w_main, w_small, seq):
    n, d = x2.shape
    tm, tn = 1024, N_MAIN // 2
    per_b = seq // tm
    mod_spec = lambda k: pl.BlockSpec((None, None, 1, d), lambda i, j: (i // per_b, k, 0, 0))
    return pl.pallas_call(
        _inproj_kernel,
        grid=(n // tm, N_MAIN // tn),
        in_specs=[
            pl.BlockSpec((tm, d), lambda i, j: (i, 0)),
            pl.BlockSpec((1, d), lambda i, j: (0, 0)),
            mod_spec(1), mod_spec(0),
            pl.BlockSpec((d, tn), lambda i, j: (0, j)),
            pl.BlockSpec((d, LANES), lambda i, j: (0, 0)),
        ],
        out_specs=[
            pl.BlockSpec((tm, tn), lambda i, j: (i, j)),
            pl.BlockSpec((tm, LANES), lambda i, j: (i, 0)),
        ],
        out_shape=[jax.ShapeDtypeStruct((n, N_MAIN), BF16),
                   jax.ShapeDtypeStruct((n, LANES), F32)],
        scratch_shapes=[pltpu.VMEM((tm, d), BF16)],
        compiler_params=_cparams(("arbitrary", "arbitrary")),
        name="in_proj",
    )(x2, g, mod4, mod4, w_main, w_small)


def _per_batch(body, n_batch_refs, n_state_refs):
    def kern(*refs):
        n_shared = len(refs) - n_batch_refs - n_state_refs - 1
        batch_in, shared = refs[:n_batch_refs], refs[n_batch_refs:n_batch_refs + n_shared]
        y_ref = refs[n_batch_refs + n_shared]
        state = refs[len(refs) - n_state_refs:]

        @pl.when(pl.program_id(1) == 0)
        def _():
            for st in state:
                st[...] = jnp.zeros_like(st)

        for bb in range(y_ref.shape[0]):
            body(*[r.at[bb] for r in batch_in], *shared, y_ref.at[bb], *[st.at[bb] for st in state])
    return kern


def _mlstm_body(qk_ref, v_ref, o_ref, sm_ref, wc_ref, bc_ref, gb_ref, go_ref, y_ref,
                tail_ref, ct_ref, n_ref, m_ref):
    L = BLK
    nqk = M_HEADS * M_QK

    x = qk_ref[...].astype(F32)
    prev = tail_ref[...]
    row8 = lax.broadcasted_iota(I32, prev.shape, 0)
    acc = x * wc_ref[CONV_W - 1:CONV_W, :] + bc_ref[...]
    for dlt in range(1, CONV_W):
        xs = pltpu.roll(x, dlt, axis=0)
        ps = pltpu.roll(prev, dlt, axis=0)
        top = jnp.where(row8 < dlt, ps, xs[0:8])
        xs = jnp.concatenate([top, xs[8:]], axis=0)
        acc = acc + xs * wc_ref[CONV_W - 1 - dlt:CONV_W - dlt, :]
    tail_ref[...] = x[L - 8:L]
    qk = acc * _sigmoid(acc)
    q = qk[:, :nqk]
    k = qk[:, nqk:] * (M_QK ** -0.5)
    kt_b = k.T.astype(BF16)

    gates = sm_ref[...] + gb_ref[...]
    row = lax.broadcasted_iota(I32, (L, L), 0)
    col = lax.broadcasted_iota(I32, (L, L), 1)
    causal = row >= col
    tri = causal.astype(BF16)
    bcum = _split_dot(tri, _log_sigmoid(gates))
    z = jnp.where(col < M_HEADS, gates, bcum)
    zt = z.T
    lane_q = lax.broadcasted_iota(I32, (L, nqk), 1)

    outs = []
    for h in range(M_HEADS):
        in_head = (lane_q >= h * M_QK) & (lane_q < (h + 1) * M_QK)
        qm = jnp.where(in_head, q, 0.0)
        qm_b = qm.astype(BF16)
        v_h = v_ref[:, h * M_V:(h + 1) * M_V]
        m_prev = m_ref[h:h + 1, 0:1]
        n_h = n_ref[h:h + 1, :]
        ct_h = ct_ref[h]
        bcol = z[:, M_HEADS + h:M_HEADS + h + 1]
        icol = z[:, h:h + 1]
        brow = zt[M_HEADS + h:M_HEADS + h + 1, :]
        irow = zt[h:h + 1, :]
        logd = jnp.where(causal, bcol - brow + irow, -jnp.inf)
        log_inter = bcol + m_prev
        m_row = jnp.maximum(log_inter, jnp.max(logd, axis=1, keepdims=True))
        dmat = jnp.exp(logd - m_row)
        w_inter = jnp.exp(log_inter - m_row)
        s = _dot(qm_b, kt_b) * dmat
        num = w_inter * _dot(qm_b, ct_h.astype(BF16)) + _dot(s.astype(BF16), v_h)
        den = (w_inter * jnp.sum(qm * n_h, axis=1, keepdims=True)
               + jnp.sum(s, axis=1, keepdims=True))
        outs.append(num / jnp.maximum(jnp.abs(den), jnp.exp(-m_row)))
        b_last = bcol[L - 1:L, :]
        log_w = b_last - bcol + icol
        m_new = jnp.maximum(b_last + m_prev, jnp.max(log_w, axis=0, keepdims=True))
        w = jnp.exp(log_w - m_new)
        decay = jnp.exp(b_last + m_prev - m_new)
        vw = (v_h.astype(F32) * w).astype(BF16)
        ct_ref[h] = decay * ct_h + _dot(kt_b, vw)
        n_ref[h:h + 1, :] = decay * n_h + jnp.sum(k * w, axis=0, keepdims=True)
        m_ref[h:h + 1, :] = jnp.broadcast_to(m_new, (1, LANES))

    og = _sigmoid(o_ref[...].astype(F32))
    for h in range(M_HEADS):
        hh = outs[h]
        ms = jnp.mean(hh * hh, axis=1, keepdims=True)
        sl = slice(h * M_V, (h + 1) * M_V)
        y = hh * lax.rsqrt(ms + EPS) * go_ref[:, sl]
        y_ref[:, sl] = (y * og[:, sl]).astype(BF16)


MIX_NB = 2


def _mixer_specs(bsz, seq):
    rows = lambda width, cb: pl.BlockSpec((MIX_NB, BLK, width), lambda b, c: (b, c, cb))
    full = lambda shp: pl.BlockSpec(shp, lambda b, c: (0,) * len(shp))
    return rows, full, (bsz // MIX_NB, seq // BLK)


def _mlstm(pm, ps, wc, bc, gb, go):
    bsz, seq, _ = pm.shape
    wd = M_HEADS * M_V
    rows, full, grid = _mixer_specs(bsz, seq)
    return pl.pallas_call(
        _per_batch(_mlstm_body, 4, 4),
        grid=grid,
        in_specs=[
            rows(wd, C_MQK // wd), rows(wd, C_MV // wd), rows(wd, C_MO // wd), rows(LANES, 0),
            full((CONV_W, 2 * M_HEADS * M_QK)), full((1, 2 * M_HEADS * M_QK)),
            full((1, LANES)), full((1, wd)),
        ],
        out_specs=rows(wd, 0),
        out_shape=jax.ShapeDtypeStruct((bsz, seq, wd), BF16),
        scratch_shapes=[
            pltpu.VMEM((MIX_NB, 8, 2 * M_HEADS * M_QK), F32),
            pltpu.VMEM((MIX_NB, M_HEADS, M_HEADS * M_QK, M_V), F32),
            pltpu.VMEM((MIX_NB, 8, M_HEADS * M_QK), F32),
            pltpu.VMEM((MIX_NB, 8, LANES), F32),
        ],
        compiler_params=_cparams(("arbitrary", "arbitrary")),
        name="mlstm",
    )(pm, pm, pm, ps, wc, bc, gb, go)


def _gla_body(qk_ref, v_ref, g_ref, sm_ref, wd_ref, bd_ref, go_ref, y_ref, s_ref, lhs_ref):
    L, C, SC = BLK, G_CHUNK, G_SUB
    nqk = G_HEADS * G_QK
    pair_k = 2 * G_QK
    pair_v = 2 * G_V
    n_pairs = G_HEADS // 2

    zdec = _dot(sm_ref[...].astype(BF16), wd_ref[...].astype(BF16)) + bd_ref[...]
    log_a = _log_sigmoid(zdec) / G_TAU
    row = lax.broadcasted_iota(I32, (L, L), 0)
    col = lax.broadcasted_iota(I32, (L, L), 1)
    same_chunk = (row // C) == (col // C)
    tri = ((row >= col) & same_chunk).astype(BF16)
    b = _split_dot(tri, log_a)
    bt = b.T

    q = qk_ref[:, :nqk].astype(F32) * (G_QK ** -0.5)
    k = qk_ref[:, nqk:].astype(F32)
    b_end = jnp.concatenate(
        [jnp.broadcast_to(b[(cc + 1) * C - 1:(cc + 1) * C, :], (C, nqk)) for cc in range(L // C)], axis=0)
    kdt = (k * jnp.exp(b_end - b)).T
    qe = q * jnp.exp(b)

    dloc = col - (row // SC) * SC
    diag_sel = jnp.where((dloc >= 0) & (dloc <= row % SC), dloc, -1)
    sub_r = (row % C) // SC
    off_sel = jnp.where(same_chunk & ((col % C) // SC < sub_r), sub_r, -1)
    diag_sel2 = jnp.concatenate([diag_sel, diag_sel], axis=1)
    off_sel2 = jnp.concatenate([off_sel, off_sel], axis=1)

    lane_s = lax.broadcasted_iota(I32, (pair_k, L), 1)
    head_k = lax.broadcasted_iota(I32, (L, pair_k), 1) // G_QK
    rblk = lax.broadcasted_iota(I32, (pair_k, pair_v), 0) // G_QK
    cblk = lax.broadcasted_iota(I32, (pair_k, pair_v), 1) // G_V
    blockdiag = rblk == cblk
    vrow = lax.broadcasted_iota(I32, (2 * L, pair_v), 0) // L
    vcol = lax.broadcasted_iota(I32, (2 * L, pair_v), 1) // G_V
    vdiag = vrow == vcol

    def stack_heads(x):
        return jnp.concatenate([jnp.where(head_k == hh, x, 0.0) for hh in range(2)], axis=0).astype(BF16)

    pair_out = []
    for p in range(n_pairs):
        kl = slice(p * pair_k, (p + 1) * pair_k)
        vl = slice(p * pair_v, (p + 1) * pair_v)
        bp, qp, kp = b[:, kl], q[:, kl], k[:, kl]
        v_p = v_ref[:, vl]
        b3 = bp.reshape(L // SC, SC, pair_k)

        def sub_row(i, b3=b3):
            return jnp.broadcast_to(b3[:, i:i + 1, :], b3.shape).reshape(L, pair_k)

        for i in range(SC):
            e = jnp.exp(jnp.minimum(bp - sub_row(i), 0.0))
            lhs_ref[i * L:(i + 1) * L, :] = (qp * e).astype(BF16)
        r = _dot_nt(lhs_ref[...], stack_heads(kp))
        att = jnp.zeros((L, 2 * L), F32)
        for i in range(SC):
            att = att + jnp.where(diag_sel2 == i, r[i * L:(i + 1) * L], 0.0)

        qs = (qp * jnp.exp(bp - sub_row(0))).astype(BF16)
        for sub in range(1, C // SC):
            bref = jnp.concatenate(
                [jnp.broadcast_to(bp[cc * C + sub * SC:cc * C + sub * SC + 1, :], (C, pair_k))
                 for cc in range(L // C)], axis=0)
            ks = kp * jnp.exp(jnp.minimum(bref - bp, 0.0))
            att = att + jnp.where(off_sel2 == sub, _dot_nt(qs, stack_heads(ks)), 0.0)

        vblk = jnp.where(vdiag, jnp.concatenate([v_p, v_p], axis=0), 0)
        intra = _dot(att.astype(BF16), vblk)

        outs = []
        for cc in range(L // C):
            r0 = cc * C
            s_p = s_ref[p]
            outs.append(_dot(qe[r0:r0 + C, kl].astype(BF16), s_p.astype(BF16)) + intra[r0:r0 + C])
            in_chunk = (lane_s >= r0) & (lane_s < r0 + C)
            kd = jnp.where(in_chunk, kdt[kl, :], 0.0).astype(BF16)
            upd = jnp.where(blockdiag, _dot(kd, v_p), 0.0)
            dcol = jnp.exp(bt[kl, r0 + C - 1:r0 + C])
            s_ref[p] = dcol * s_p + upd
        pair_out.append(outs)

    gate = g_ref[...].astype(F32)
    gate = gate * _sigmoid(gate)
    for p in range(n_pairs):
        o_p = jnp.concatenate(pair_out[p], axis=0)
        for hh in range(2):
            h = 2 * p + hh
            oh = o_p[:, hh * G_V:(hh + 1) * G_V]
            ms = jnp.mean(oh * oh, axis=1, keepdims=True)
            sl = slice(h * G_V, (h + 1) * G_V)
            y = oh * lax.rsqrt(ms + EPS) * go_ref[:, sl]
            y_ref[:, sl] = (y * gate[:, sl]).astype(BF16)


def _gla(pm, ps, wd, bd, go):
    bsz, seq, _ = pm.shape
    w = G_HEADS * G_V
    nqk = G_HEADS * G_QK
    rows, full, grid = _mixer_specs(bsz, seq)
    return pl.pallas_call(
        _per_batch(_gla_body, 4, 2),
        grid=grid,
        in_specs=[
            rows(w, C_GQK // w), rows(w, C_GV // w), rows(w, C_GG // w), rows(LANES, 0),
            full((LANES, nqk)), full((1, nqk)), full((1, w)),
        ],
        out_specs=rows(w, 0),
        out_shape=jax.ShapeDtypeStruct((bsz, seq, w), BF16),
        scratch_shapes=[
            pltpu.VMEM((MIX_NB, G_HEADS // 2, 2 * G_QK, 2 * G_V), F32),
            pltpu.VMEM((MIX_NB, G_SUB * BLK, 2 * G_QK), BF16),
        ],
        compiler_params=_cparams(("arbitrary", "arbitrary")),
        name="gla",
    )(pm, pm, pm, ps, wd, bd, go)


def _group_mean_sq(x, ind):
    return _split_dot_rhs(x * x, ind)


def _split_dot_rhs(a_f32, b_bf16):
    hi = a_f32.astype(BF16)
    lo = (a_f32 - hi.astype(F32)).astype(BF16)
    return _dot(hi, b_bf16) + _dot(lo, b_bf16)


def _swa_body(q_ref, kc_ref, kp_ref, vc_ref, vp_ref, bias_ref, gq_ref, gk_ref, sink_ref, ind_ref,
              y_ref):
    T = BLK
    heads_per_tile = LANES // A_HD
    n_tiles = A_HEADS // heads_per_tile
    grp = A_HEADS // A_KV
    first_key = jnp.where(pl.program_id(1) == 0, T, 0)
    ind = ind_ref[...]

    qn = []
    for t in range(n_tiles):
        sl = slice(t * LANES, (t + 1) * LANES)
        qt = q_ref[:, sl].astype(F32)
        ms = _group_mean_sq(qt, ind)
        qn.append((qt * lax.rsqrt(ms + EPS) * gq_ref[:, sl]).astype(BF16))
    kk = jnp.concatenate([kp_ref[...], kc_ref[...]], axis=0).astype(F32)
    kn = kk * lax.rsqrt(_group_mean_sq(kk, ind) + EPS) * gk_ref[...]
    vv = jnp.concatenate([vp_ref[...], vc_ref[...]], axis=0).astype(F32)
    k_sw = pltpu.roll(kn, A_HD, axis=1)
    v_sw = pltpu.roll(vv, A_HD, axis=1)
    half = lax.broadcasted_iota(I32, (2 * T, LANES), 1) // A_HD

    j = lax.broadcasted_iota(I32, (T, 2 * T), 0)
    s = lax.broadcasted_iota(I32, (T, 2 * T), 1)
    dist = j + T - s
    mask = (dist >= 0) & (dist < WINDOW) & (s >= first_key)

    acc = [None] * n_tiles
    for g in range(A_KV):
        for p in range(heads_per_tile):
            ksrc, vsrc = (kn, vv) if g == p else (k_sw, v_sw)
            kz = jnp.where(half == p, ksrc, 0.0).astype(BF16)
            vz = jnp.where(half == p, vsrc, 0.0).astype(BF16)
            heads = [g * grp + heads_per_tile * u + p for u in range(grp // heads_per_tile)]
            qs = jnp.concatenate([qn[h // heads_per_tile] for h in heads], axis=0)
            sc_all = _dot_nt(qs, kz)
            probs, dens = [], []
            for u, h in enumerate(heads):
                sc = jnp.where(mask, sc_all[u * T:(u + 1) * T] + bias_ref[h], -jnp.inf)
                sink = sink_ref[h]
                m = jnp.maximum(jnp.max(sc, axis=1, keepdims=True), sink)
                pr = jnp.exp(sc - m)
                dens.append(jnp.sum(pr, axis=1, keepdims=True) + jnp.exp(sink - m))
                probs.append(pr.astype(BF16))
            o_all = _dot(jnp.concatenate(probs, axis=0), vz)
            for u, h in enumerate(heads):
                o = o_all[u * T:(u + 1) * T] / dens[u]
                t = h // heads_per_tile
                acc[t] = o if acc[t] is None else acc[t] + o
    for t in range(n_tiles):
        y_ref[:, t * LANES:(t + 1) * LANES] = acc[t].astype(BF16)


def _swa(pm, bias, gq, gk, sinks, ind):
    bsz, seq, _ = pm.shape
    wq = A_HEADS * A_HD
    rows, full, grid = _mixer_specs(bsz, seq)
    cur = lambda cb: rows(LANES, cb)
    prv = lambda cb: pl.BlockSpec((MIX_NB, BLK, LANES), lambda b, c: (b, jnp.maximum(c - 1, 0), cb))
    return pl.pallas_call(
        _per_batch(_swa_body, 5, 0),
        grid=grid,
        in_specs=[
            rows(wq, C_AQ // wq),
            cur(C_AK // LANES), prv(C_AK // LANES), cur(C_AV // LANES), prv(C_AV // LANES),
            full((A_HEADS, BLK, 2 * BLK)), full((1, wq)), full((1, LANES)),
            pl.BlockSpec(memory_space=pltpu.SMEM),
            full((LANES, LANES)),
        ],
        out_specs=rows(wq, 0),
        out_shape=jax.ShapeDtypeStruct((bsz, seq, wq), BF16),
        compiler_params=_cparams(("arbitrary", "arbitrary")),
        name="swa",
    )(pm, pm, pm, pm, pm, bias, gq, gk, sinks, ind)


def _outproj_kernel(ym_ref, yg_ref, ya_ref, w_ref, x_ref, gt_ref, o_ref):
    y = jnp.concatenate([ym_ref[...], yg_ref[...], ya_ref[...]], axis=1)
    o_ref[...] = x_ref[...] + gt_ref[...] * _dot(y, w_ref[...])


def _outproj(ym, yg, ya, w, x2, mod4, seq):
    n, d = x2.shape
    tm, tn = 1024, d
    per_b = seq // tm
    return pl.pallas_call(
        _outproj_kernel,
        grid=(n // tm, d // tn),
        in_specs=[
            pl.BlockSpec((tm, ym.shape[1]), lambda i, j: (i, 0)),
            pl.BlockSpec((tm, yg.shape[1]), lambda i, j: (i, 0)),
            pl.BlockSpec((tm, ya.shape[1]), lambda i, j: (i, 0)),
            pl.BlockSpec((d, tn), lambda i, j: (0, j), pipeline_mode=pl.Buffered(1)),
            pl.BlockSpec((tm, tn), lambda i, j: (i, j)),
            pl.BlockSpec((None, None, 1, tn), lambda i, j: (i // per_b, 2, 0, j)),
        ],
        out_specs=pl.BlockSpec((tm, tn), lambda i, j: (i, j)),
        out_shape=jax.ShapeDtypeStruct((n, d), F32),
        compiler_params=_cparams(("arbitrary", "arbitrary")),
        name="out_proj",
    )(ym, yg, ya, w, x2, mod4)


def _swiglu_tile(h, wg_ref, wu_ref, wd_ref):
    gate = _dot(h, wg_ref[...].astype(BF16))
    up = _dot(h, wu_ref[...].astype(BF16))
    a = (gate * _sigmoid(gate) * up).astype(BF16)
    return _dot(a, wd_ref[...].astype(BF16))


def _ffn_kernel(x_ref, g_ref, sc_ref, sh_ref, gt_ref, wg_ref, wu_ref, wd_ref, o_ref, h_ref):
    f = pl.program_id(1)

    @pl.when(f == 0)
    def _():
        _norm_mod_rows(x_ref, g_ref, sc_ref, sh_ref, [h_ref])
        o_ref[...] = jnp.zeros_like(o_ref)

    o_ref[...] += _swiglu_tile(h_ref[...], wg_ref, wu_ref, wd_ref)

    @pl.when(f == pl.num_programs(1) - 1)
    def _():
        o_ref[...] = x_ref[...] + gt_ref[...] * o_ref[...]


def _ffn(x2, g, mod4, wg, wu, wd, seq):
    n, d = x2.shape
    dff = wg.shape[1]
    tm, tf = 1024, 512
    per_b = seq // tm
    mod_spec = lambda k: pl.BlockSpec((None, None, 1, d), lambda i, f: (i // per_b, k, 0, 0))
    return pl.pallas_call(
        _ffn_kernel,
        grid=(n // tm, dff // tf),
        in_specs=[
            pl.BlockSpec((tm, d), lambda i, f: (i, 0), pipeline_mode=pl.Buffered(1)),
            pl.BlockSpec((1, d), lambda i, f: (0, 0)),
            mod_spec(4), mod_spec(3), mod_spec(5),
            pl.BlockSpec((d, tf), lambda i, f: (0, f)),
            pl.BlockSpec((d, tf), lambda i, f: (0, f)),
            pl.BlockSpec((tf, d), lambda i, f: (f, 0)),
        ],
        out_specs=pl.BlockSpec((tm, d), lambda i, f: (i, 0), pipeline_mode=pl.Buffered(1)),
        out_shape=jax.ShapeDtypeStruct((n, d), F32),
        scratch_shapes=[pltpu.VMEM((tm, d), BF16)],
        compiler_params=_cparams(("arbitrary", "arbitrary")),
        name="ffn_dense",
    )(x2, g, mod4, mod4, mod4, wg, wu, wd)


def _pack_bf16_pairs(h):
    w = h.shape[1] // 2
    bits = lax.bitcast_convert_type(h.astype(BF16).astype(F32), jnp.uint32)
    return (bits[:, :w] >> 16) | (bits[:, w:] & jnp.uint32(0xFFFF0000))


def _unpack_bf16_pairs(u):
    lo = lax.bitcast_convert_type(u << 16, F32)
    hi = lax.bitcast_convert_type(u & jnp.uint32(0xFFFF0000), F32)
    return jnp.concatenate([lo, hi], axis=1).astype(BF16)


def _router_kernel(x_ref, g_ref, sc_ref, sh_ref, wh_ref, wl_ref, tri_ref, hp_ref, meta_ref, cnt_ref,
                   h_ref, run_ref):
    @pl.when(pl.program_id(0) == 0)
    def _():
        run_ref[...] = jnp.zeros_like(run_ref)

    g, sc, sh = g_ref[...], sc_ref[...], sh_ref[...]

    def norm_rows(r, carry):
        rows = pl.ds(pl.multiple_of(r * NORM_ROWS, NORM_ROWS), NORM_ROWS)
        hr = _norm_mod(x_ref[rows, :], g, sc, sh)
        h_ref[rows, :] = hr
        hp_ref[rows, :] = _pack_bf16_pairs(hr)
        return carry

    lax.fori_loop(0, x_ref.shape[0] // NORM_ROWS, norm_rows, 0, unroll=8)
    h = h_ref[...]
    h_hi = h.astype(BF16)
    h_lo = (h - h_hi.astype(F32)).astype(BF16)
    logits = _dot(h_hi, wh_ref[...]) + _dot(h_lo, wh_ref[...]) + _dot(h_hi, wl_ref[...])
    lane = lax.broadcasted_iota(I32, logits.shape, 1).astype(F32)
    l1 = jnp.where(lane < N_EXPERTS, logits, -jnp.inf)
    m1 = jnp.max(l1, axis=1, keepdims=True)
    i1 = jnp.min(jnp.where(l1 == m1, lane, float(LANES)), axis=1, keepdims=True)
    l2 = jnp.where(lane == i1, -jnp.inf, l1)
    m2 = jnp.max(l2, axis=1, keepdims=True)
    i2 = jnp.min(jnp.where(l2 == m2, lane, float(LANES)), axis=1, keepdims=True)
    e2 = jnp.exp(m2 - m1)
    w1 = 1.0 / (1.0 + e2)
    w2 = e2 / (1.0 + e2)
    member = jnp.where((lane == i1) | (lane == i2), 1.0, 0.0)
    run = run_ref[0:1, :]
    rank = _dot(tri_ref[...], member.astype(BF16)) + run
    p1 = jnp.sum(jnp.where(lane == i1, rank, 0.0), axis=1, keepdims=True)
    p2 = jnp.sum(jnp.where(lane == i2, rank, 0.0), axis=1, keepdims=True)
    run = run + jnp.sum(member, axis=0, keepdims=True)
    run_ref[0:1, :] = run
    cnt_ref[...] = jnp.broadcast_to(run, cnt_ref.shape)
    meta = jnp.zeros(logits.shape, F32)
    for idx, val in enumerate([i1, i2, w1, w2, p1, p2]):
        meta = jnp.where(lane == float(idx), val, meta)
    meta_ref[...] = meta


def _router(x2, g, mod4, wr_hi, wr_lo, seq):
    n, d = x2.shape
    tm = 512
    per_b = seq // tm
    tri = jnp.asarray(np.tril(np.ones((tm, tm), np.float32), -1), BF16)
    mod_spec = lambda k: pl.BlockSpec((None, None, 1, d), lambda i: (i // per_b, k, 0, 0))
    return pl.pallas_call(
        _router_kernel,
        grid=(n // tm,),
        in_specs=[
            pl.BlockSpec((tm, d), lambda i: (i, 0)),
            pl.BlockSpec((1, d), lambda i: (0, 0)),
            mod_spec(4), mod_spec(3),
            pl.BlockSpec((d, LANES), lambda i: (0, 0)),
            pl.BlockSpec((d, LANES), lambda i: (0, 0)),
            pl.BlockSpec((tm, tm), lambda i: (0, 0)),
        ],
        out_specs=[
            pl.BlockSpec((tm, d // 2), lambda i: (i, 0)),
            pl.BlockSpec((tm, LANES), lambda i: (i, 0)),
            pl.BlockSpec((8, LANES), lambda i: (0, 0)),
        ],
        out_shape=[jax.ShapeDtypeStruct((n, d // 2), jnp.uint32),
                   jax.ShapeDtypeStruct((n, LANES), F32),
                   jax.ShapeDtypeStruct((8, LANES), F32)],
        scratch_shapes=[pltpu.VMEM((tm, d), F32), pltpu.VMEM((8, LANES), F32)],
        compiler_params=_cparams(("arbitrary",)),
        name="router",
    )(x2, g, mod4, mod4, wr_hi, wr_lo, tri)


MOE_TM = 1024
DISP_TM = 512


def _dispatch_kernel(dest_hbm, hp_ref, xs_in_hbm, xs_hbm, idx_ref, sem_i, sem_r):
    del xs_in_hbm
    i = pl.program_id(0)
    n_idx = 2 * DISP_TM
    cp = pltpu.make_async_copy(dest_hbm.at[pl.ds(pl.multiple_of(i * n_idx, n_idx), n_idx)], idx_ref, sem_i)
    cp.start()
    cp.wait()

    def row_copy(r, k):
        return pltpu.make_async_copy(hp_ref.at[pl.ds(r, 1), :],
                                     xs_hbm.at[pl.ds(idx_ref[k * DISP_TM + r], 1), :], sem_r)

    def start(r, c):
        row_copy(r, 0).start()
        row_copy(r, 1).start()
        return c

    def wait(r, c):
        row_copy(r, 0).wait()
        row_copy(r, 1).wait()
        return c

    lax.fori_loop(0, DISP_TM, start, 0, unroll=8)
    lax.fori_loop(0, DISP_TM, wait, 0, unroll=8)


def _dispatch(dest_tiles, hp, n_rows):
    n, w = hp.shape
    xs0 = jnp.zeros((n_rows, w), jnp.uint32)
    return pl.pallas_call(
        _dispatch_kernel,
        grid=(n // DISP_TM,),
        in_specs=[
            pl.BlockSpec(memory_space=pl.ANY),
            pl.BlockSpec((DISP_TM, w), lambda i: (i, 0)),
            pl.BlockSpec(memory_space=pl.ANY),
        ],
        out_specs=pl.BlockSpec(memory_space=pl.ANY),
        out_shape=jax.ShapeDtypeStruct((n_rows, w), jnp.uint32),
        scratch_shapes=[
            pltpu.SMEM((2 * DISP_TM,), I32),
            pltpu.SemaphoreType.DMA(()),
            pltpu.SemaphoreType.DMA(()),
        ],
        input_output_aliases={2: 0},
        compiler_params=_cparams(("arbitrary",)),
        name="moe_dispatch",
    )(dest_tiles, hp, xs0)


def _gather_rows(idx_ref, src_hbm, dst_ref, sem, n_rows, dst_off=0):
    def row_copy(r):
        return pltpu.make_async_copy(src_hbm.at[pl.ds(idx_ref[dst_off + r], 1), :],
                                     dst_ref.at[pl.ds(r, 1), :], sem)

    def start(r, c):
        row_copy(r).start()
        return c

    def wait(r, c):
        row_copy(r).wait()
        return c

    lax.fori_loop(0, n_rows, start, 0, unroll=8)
    lax.fori_loop(0, n_rows, wait, 0, unroll=8)


def _moe_kernel(te_ref, nu_ref, xs_ref, wg_ref, wu_ref, wd_ref, o_ref, xb_ref):
    t = pl.program_id(0)
    f = pl.program_id(1)
    used = t < nu_ref[0]

    @pl.when(used & (f == 0))
    def _():
        xb_ref[...] = _unpack_bf16_pairs(xs_ref[...])
        o_ref[...] = jnp.zeros_like(o_ref)

    @pl.when(used)
    def _():
        o_ref[...] += _swiglu_tile(xb_ref[...], wg_ref, wu_ref, wd_ref)

    @pl.when(jnp.logical_not(used) & (f == 0))
    def _():
        o_ref[...] = jnp.zeros_like(o_ref)


def _moe(tile_expert, n_used, xs, wg, wu, wd):
    d = wg.shape[1]
    n_tiles = tile_expert.shape[0]
    dff = wg.shape[2]
    tf = 512
    nf = dff // tf

    def f_blk(t, f, nu):
        return jnp.where(t < nu[0], f, nf - 1)

    grid_spec = pltpu.PrefetchScalarGridSpec(
        num_scalar_prefetch=2,
        grid=(n_tiles, nf),
        in_specs=[
            pl.BlockSpec((MOE_TM, d // 2), lambda t, f, te, nu: (jnp.minimum(t, nu[0] - 1), 0)),
            pl.BlockSpec((None, d, tf), lambda t, f, te, nu: (te[t], 0, f_blk(t, f, nu))),
            pl.BlockSpec((None, d, tf), lambda t, f, te, nu: (te[t], 0, f_blk(t, f, nu))),
            pl.BlockSpec((None, tf, d), lambda t, f, te, nu: (te[t], f_blk(t, f, nu), 0)),
        ],
        out_specs=pl.BlockSpec((MOE_TM, d), lambda t, f, te, nu: (t, 0), pipeline_mode=pl.Buffered(1)),
        scratch_shapes=[pltpu.VMEM((MOE_TM, d), BF16)],
    )
    return pl.pallas_call(
        _moe_kernel,
        grid_spec=grid_spec,
        out_shape=jax.ShapeDtypeStruct((n_tiles * MOE_TM, d), F32),
        compiler_params=_cparams(("arbitrary", "arbitrary")),
        name="moe_experts",
    )(tile_expert, n_used, xs, wg, wu, wd)


COMB_TM = 512


def _combine_kernel(dest_hbm, y_hbm, x_ref, meta_ref, gt_ref, o_ref, idx_ref, y0_ref, y1_ref, sem_i, sem_r):
    i = pl.program_id(0)
    cp = pltpu.make_async_copy(
        dest_hbm.at[pl.ds(pl.multiple_of(i * 2 * COMB_TM, 2 * COMB_TM), 2 * COMB_TM)], idx_ref, sem_i)
    cp.start()
    cp.wait()
    _gather_rows(idx_ref, y_hbm, y0_ref, sem_r, COMB_TM, 0)
    _gather_rows(idx_ref, y_hbm, y1_ref, sem_r, COMB_TM, COMB_TM)
    w1 = meta_ref[:, 2:3]
    w2 = meta_ref[:, 3:4]
    o_ref[...] = x_ref[...] + gt_ref[...] * (w1 * y0_ref[...] + w2 * y1_ref[...])


def _combine(dest, y_sorted, x2, meta, mod4, seq):
    n, d = x2.shape
    tm = COMB_TM
    per_b = seq // tm
    return pl.pallas_call(
        _combine_kernel,
        grid=(n // tm,),
        in_specs=[
            pl.BlockSpec(memory_space=pl.ANY),
            pl.BlockSpec(memory_space=pl.ANY),
            pl.BlockSpec((tm, d), lambda i: (i, 0)),
            pl.BlockSpec((tm, LANES), lambda i: (i, 0)),
            pl.BlockSpec((None, None, 1, d), lambda i: (i // per_b, 5, 0, 0)),
        ],
        out_specs=pl.BlockSpec((tm, d), lambda i: (i, 0)),
        out_shape=jax.ShapeDtypeStruct((n, d), F32),
        scratch_shapes=[
            pltpu.SMEM((2 * tm,), I32),
            pltpu.VMEM((tm, d), F32),
            pltpu.VMEM((tm, d), F32),
            pltpu.SemaphoreType.DMA(()),
            pltpu.SemaphoreType.DMA(()),
        ],
        compiler_params=_cparams(("arbitrary",)),
        name="moe_combine",
    )(dest, y_sorted, x2, meta, mod4)


def _moe_layer(x2, g, mod4, w_router, wg, wu, wd, seq):
    n, d = x2.shape
    wr = jnp.zeros((d, LANES), F32).at[:, :N_EXPERTS].set(w_router)
    wr_hi = wr.astype(BF16)
    wr_lo = (wr - wr_hi.astype(F32)).astype(BF16)
    hp, meta, cnt = _router(x2, g, mod4, wr_hi, wr_lo, seq)

    n_tiles = 2 * n // MOE_TM + N_EXPERTS
    counts = cnt[0, :N_EXPERTS].astype(I32)
    tiles_per = (counts + MOE_TM - 1) // MOE_TM
    tile_end = jnp.cumsum(tiles_per)
    row_start = (tile_end - tiles_per) * MOE_TM
    n_used = tile_end[-1]
    idx = meta[:, 0:2].astype(I32)
    pos = meta[:, 4:6].astype(I32)
    start_of = sum(jnp.where(idx == e, row_start[e], 0) for e in range(N_EXPERTS))
    dest = start_of + pos
    tile_ids = jnp.minimum(jnp.arange(n_tiles, dtype=I32), n_used - 1)
    tile_expert = jnp.sum(tile_ids[:, None] >= tile_end[None, :], axis=1).astype(I32)
    assert COMB_TM == DISP_TM
    dest_tiles = dest.reshape(n // COMB_TM, COMB_TM, 2).transpose(0, 2, 1).reshape(-1)

    xs = _dispatch(dest_tiles, hp, n_tiles * MOE_TM)
    y_sorted = _moe(tile_expert, n_used.reshape(1).astype(I32), xs, wg, wu, wd)
    return _combine(dest_tiles, y_sorted, x2, meta, mod4, seq)


def _t5_bucket(dist):
    max_exact = N_BUCKETS // 2
    d = np.maximum(dist, 1).astype(np.float32)
    large = max_exact + (np.log(d / max_exact) / np.log(MAX_DIST / max_exact)
                         * (N_BUCKETS - max_exact)).astype(np.int32)
    large = np.minimum(large, N_BUCKETS - 1)
    return np.where(dist < max_exact, dist, large).astype(np.int32)


def _pack_w_in(w):
    splits = [M_HEADS * M_QK, M_HEADS * M_QK, M_HEADS * M_V, M_HEADS * M_V, 2 * M_HEADS,
              G_HEADS * G_QK, G_HEADS * G_QK, G_HEADS * G_V, G_HEADS * G_V, G_RANK,
              A_HEADS * A_HD, A_KV * A_HD, A_KV * A_HD]
    offs = np.concatenate([[0], np.cumsum(splits)])
    seg = lambda i: w[:, offs[i]:offs[i + 1]]
    main = jnp.concatenate([seg(i) for i in (0, 1, 2, 3, 5, 6, 7, 8, 10, 11, 12)], axis=1)
    small = jnp.concatenate(
        [seg(4), seg(9), jnp.zeros((w.shape[0], LANES - 2 * M_HEADS - G_RANK), w.dtype)], axis=1)
    return main.astype(BF16), small.astype(BF16)


def kernel(x, c, w_ada, b_ada, g_mix_norm, g_ffn_norm, w_in, b_gates_m, w_conv_m, b_conv_m, g_out_m,
           w_gla_decay, b_gla_decay, g_out_g, g_qnorm, g_knorm, sinks, rel_bias, w_out, w_ffn_gate,
           w_ffn_up, w_ffn_down, w_router, w_moe_gate, w_moe_up, w_moe_down):
    bsz, seq, d = x.shape
    depth = w_ada.shape[0]
    n = bsz * seq
    assert seq % 1024 == 0 and d == 2048 and n % MOE_TM == 0 and bsz % MIX_NB == 0

    mod = _ada(c, w_ada, b_ada)

    jj = np.arange(BLK)[:, None]
    ss = np.arange(2 * BLK)[None, :]
    buckets = _t5_bucket(np.clip(jj + BLK - ss, 0, None))
    onehot = jnp.asarray(np.eye(N_BUCKETS, dtype=np.float32)[buckets.reshape(-1)])
    bias = jnp.einsum("pb,bh->hp", onehot, rel_bias.astype(F32),
                      precision=lax.Precision.HIGHEST).reshape(A_HEADS, BLK, 2 * BLK)
    ind = jnp.asarray(np.kron(np.eye(LANES // A_HD), np.ones((A_HD, A_HD))) / A_HD, BF16)

    x2 = x.reshape(n, d)
    for l in range(depth):
        mod4 = mod[l].reshape(bsz, 6, 1, d)
        w_main, w_small = _pack_w_in(w_in[l])
        pm, ps = _inproj(x2, g_mix_norm[l].reshape(1, d), mod4, w_main, w_small, seq)
        pm = pm.reshape(bsz, seq, -1)
        ps = ps.reshape(bsz, seq, -1)

        gb = jnp.zeros((1, LANES), F32).at[0, :2 * M_HEADS].set(b_gates_m[l].reshape(-1))
        ym = _mlstm(pm, ps, w_conv_m[l], b_conv_m[l].reshape(1, -1), gb, g_out_m[l].reshape(1, -1))
        wdec = jnp.zeros((LANES, G_HEADS * G_QK), F32).at[S_GA:S_GA + G_RANK].set(w_gla_decay[l])
        yg = _gla(pm, ps, wdec, b_gla_decay[l].reshape(1, -1), g_out_g[l].reshape(1, -1))
        gq = jnp.tile(g_qnorm[l], A_HEADS).reshape(1, -1) * (A_HD ** -0.5)
        gk = jnp.tile(g_knorm[l], A_KV).reshape(1, -1)
        ya = _swa(pm, bias, gq, gk, sinks[l], ind)
        x2 = _outproj(ym.reshape(n, -1), yg.reshape(n, -1), ya.reshape(n, -1), w_out[l].astype(BF16),
                      x2, mod4, seq)

        gf = g_ffn_norm[l].reshape(1, d)
        if l % 2 == 0:
            j = l // 2
            x2 = _ffn(x2, gf, mod4, w_ffn_gate[j], w_ffn_up[j], w_ffn_down[j], seq)
        else:
            j = l // 2
            x2 = _moe_layer(x2, gf, mod4, w_router[j], w_moe_gate[j], w_moe_up[j], w_moe_down[j], seq)
    return x2.reshape(bsz, seq, d)
```

```python
import functools

import numpy as np
import jax
import jax.numpy as jnp
from jax import lax
from jax.experimental import pallas as pl
from jax.experimental.pallas import tpu as pltpu

F32 = jnp.float32
BF16 = jnp.bfloat16
I32 = jnp.int32

M_HEADS = 4
M_QK = 64
M_V = 128
CONV_W = 4
G_HEADS = 4
G_QK = 64
G_V = 128
G_RANK = 16
G_TAU = 16.0
G_CHUNK = 64
G_SUB = 16
A_HD = 64
A_HEADS = 16
A_KV = 2
WINDOW = 128
N_BUCKETS = 32
MAX_DIST = 128
N_EXPERTS = 8
EPS = 1e-6
LOG2E = 1.4426950408889634

LANES = 128
BLK = 128
VMEM_LIMIT = 56 * 1024 * 1024

C_MQK, C_MV, C_MO = 0, 512, 1024
C_GQK, C_GV, C_GG = 1536, 2048, 2560
C_AQ, C_AK, C_AV = 3072, 4096, 4224
N_MAIN = 4352
S_GA = 8


def _cparams(sem):
    return pltpu.CompilerParams(dimension_semantics=sem, vmem_limit_bytes=VMEM_LIMIT)


def _dot(a, b):
    return jnp.dot(a, b, preferred_element_type=F32)


def _dot_nt(a, b):
    return lax.dot_general(a, b, (((1,), (1,)), ((), ())), preferred_element_type=F32)


def _split_dot(a, b_f32):
    hi = b_f32.astype(BF16)
    lo = (b_f32 - hi.astype(F32)).astype(BF16)
    return _dot(a, hi) + _dot(a, lo)


def _sigmoid(x):
    return 1.0 / (1.0 + jnp.exp(-x))


def _log_sigmoid(x):
    return jnp.minimum(x, 0.0) - jnp.log(1.0 + jnp.exp(-jnp.abs(x)))


def _norm_mod(x, g, sc, sh):
    ms = jnp.mean(x * x, axis=-1, keepdims=True)
    return (x * lax.rsqrt(ms + EPS) * g) * (1.0 + sc) + sh


NORM_ROWS = 16


def _norm_mod_rows(x_ref, g_ref, sc_ref, sh_ref, out_refs):
    g, sc, sh = g_ref[...], sc_ref[...], sh_ref[...]

    def body(r, carry):
        rows = pl.ds(pl.multiple_of(r * NORM_ROWS, NORM_ROWS), NORM_ROWS)
        h = _norm_mod(x_ref[rows, :], g, sc, sh)
        for ref in out_refs:
            ref[rows, :] = h.astype(ref.dtype)
        return carry

    lax.fori_loop(0, x_ref.shape[0] // NORM_ROWS, body, 0, unroll=8)


def _ada_kernel(c_ref, w_ref, b_ref, o_ref):
    c = c_ref[...]
    cond = (c * _sigmoid(c)).astype(BF16)
    o_ref[...] = _dot(cond, w_ref[...].astype(BF16)) + b_ref[...]


def _ada(c, w_ada, b_ada):
    depth, d, n6 = w_ada.shape
    bsz = c.shape[0]
    tn = 1024
    return pl.pallas_call(
        _ada_kernel,
        grid=(depth, n6 // tn),
        in_specs=[
            pl.BlockSpec((bsz, d), lambda l, n: (0, 0)),
            pl.BlockSpec((None, d, tn), lambda l, n: (l, 0, n)),
            pl.BlockSpec((None, 1, tn), lambda l, n: (l, 0, n)),
        ],
        out_specs=pl.BlockSpec((None, bsz, tn), lambda l, n: (l, 0, n)),
        out_shape=jax.ShapeDtypeStruct((depth, bsz, n6), F32),
        compiler_params=_cparams(("arbitrary", "arbitrary")),
        name="ada_mod",
    )(c, w_ada, b_ada.reshape(depth, 1, n6))


def _inproj_kernel(x_ref, g_ref, sc_ref, sh_ref, wm_ref, ws_ref, om_ref, os_ref, h_ref):
    @pl.when(pl.program_id(1) == 0)
    def _():
        _norm_mod_rows(x_ref, g_ref, sc_ref, sh_ref, [h_ref])
        os_ref[...] = _dot(h_ref[...], ws_ref[...])

    om_ref[...] = _dot(h_ref[...], wm_ref[...]).astype(BF16)


def _inproj(x2, g, mod4, w_main, w_small, seq):
    n, d = x2.shape
    tm, tn = 1024, N_MAIN // 2
    per_b = seq // tm
    mod_spec = lambda k: pl.BlockSpec((None, None, 1, d), lambda i, j: (i // per_b, k, 0, 0))
    return pl.pallas_call(
        _inproj_kernel,
        grid=(n // tm, N_MAIN // tn),
        in_specs=[
            pl.BlockSpec((tm, d), lambda i, j: (i, 0)),
            pl.BlockSpec((1, d), lambda i, j: (0, 0)),
            mod_spec(1), mod_spec(0),
            pl.BlockSpec((d, tn), lambda i, j: (0, j)),
            pl.BlockSpec((d, LANES), lambda i, j: (0, 0)),
        ],
        out_specs=[
            pl.BlockSpec((tm, tn), lambda i, j: (i, j)),
            pl.BlockSpec((tm, LANES), lambda i, j: (i, 0)),
        ],
        out_shape=[jax.ShapeDtypeStruct((n, N_MAIN), BF16),
                   jax.ShapeDtypeStruct((n, LANES), F32)],
        scratch_shapes=[pltpu.VMEM((tm, d), BF16)],
        compiler_params=_cparams(("arbitrary", "arbitrary")),
        name="in_proj",
    )(x2, g, mod4, mod4, w_main, w_small)


def _per_batch(body, n_batch_refs, n_state_refs):
    def kern(*refs):
        n_shared = len(refs) - n_batch_refs - n_state_refs - 1
        batch_in, shared = refs[:n_batch_refs], refs[n_batch_refs:n_batch_refs + n_shared]
        y_ref = refs[n_batch_refs + n_shared]
        state = refs[len(refs) - n_state_refs:]

        @pl.when(pl.program_id(1) == 0)
        def _():
            for st in state:
                st[...] = jnp.zeros_like(st)

        for bb in range(y_ref.shape[0]):
            body(*[r.at[bb] for r in batch_in], *shared, y_ref.at[bb], *[st.at[bb] for st in state])
    return kern


def _mlstm_body(qk_ref, v_ref, o_ref, sm_ref, wc_ref, bc_ref, gb_ref, go_ref, y_ref,
                tail_ref, ct_ref, n_ref, m_ref):
    L = BLK
    nqk = M_HEADS * M_QK

    x = qk_ref[...].astype(F32)
    prev = tail_ref[...]
    row8 = lax.broadcasted_iota(I32, prev.shape, 0)
    acc = x * wc_ref[CONV_W - 1:CONV_W, :] + bc_ref[...]
    for dlt in range(1, CONV_W):
        xs = pltpu.roll(x, dlt, axis=0)
        ps = pltpu.roll(prev, dlt, axis=0)
        top = jnp.where(row8 < dlt, ps, xs[0:8])
        xs = jnp.concatenate([top, xs[8:]], axis=0)
        acc = acc + xs * wc_ref[CONV_W - 1 - dlt:CONV_W - dlt, :]
    tail_ref[...] = x[L - 8:L]
    qk = acc * _sigmoid(acc)
    q = qk[:, :nqk]
    k = qk[:, nqk:] * (M_QK ** -0.5)
    kt_b = k.T.astype(BF16)

    gates = sm_ref[...] + gb_ref[...]
    row = lax.broadcasted_iota(I32, (L, L), 0)
    col = lax.broadcasted_iota(I32, (L, L), 1)
    causal = row >= col
    tri = causal.astype(BF16)
    bcum = _split_dot(tri, _log_sigmoid(gates))
    z = jnp.where(col < M_HEADS, gates, bcum)
    zt = z.T
    lane_q = lax.broadcasted_iota(I32, (L, nqk), 1)

    outs = []
    for h in range(M_HEADS):
        in_head = (lane_q >= h * M_QK) & (lane_q < (h + 1) * M_QK)
        qm = jnp.where(in_head, q, 0.0)
        qm_b = qm.astype(BF16)
        v_h = v_ref[:, h * M_V:(h + 1) * M_V]
        m_prev = m_ref[h:h + 1, 0:1]
        n_h = n_ref[h:h + 1, :]
        ct_h = ct_ref[h]
        bcol = z[:, M_HEADS + h:M_HEADS + h + 1]
        icol = z[:, h:h + 1]
        brow = zt[M_HEADS + h:M_HEADS + h + 1, :]
        irow = zt[h:h + 1, :]
        logd = jnp.where(causal, bcol - brow + irow, -jnp.inf)
        log_inter = bcol + m_prev
        m_row = jnp.maximum(log_inter, jnp.max(logd, axis=1, keepdims=True))
        dmat = jnp.exp(logd - m_row)
        w_inter = jnp.exp(log_inter - m_row)
        s = _dot(qm_b, kt_b) * dmat
        num = w_inter * _dot(qm_b, ct_h.astype(BF16)) + _dot(s.astype(BF16), v_h)
        den = (w_inter * jnp.sum(qm * n_h, axis=1, keepdims=True)
               + jnp.sum(s, axis=1, keepdims=True))
        outs.append(num / jnp.maximum(jnp.abs(den), jnp.exp(-m_row)))
        b_last = bcol[L - 1:L, :]
        log_w = b_last - bcol + icol
        m_new = jnp.maximum(b_last + m_prev, jnp.max(log_w, axis=0, keepdims=True))
        w = jnp.exp(log_w - m_new)
        decay = jnp.exp(b_last + m_prev - m_new)
        vw = (v_h.astype(F32) * w).astype(BF16)
        ct_ref[h] = decay * ct_h + _dot(kt_b, vw)
        n_ref[h:h + 1, :] = decay * n_h + jnp.sum(k * w, axis=0, keepdims=True)
        m_ref[h:h + 1, :] = jnp.broadcast_to(m_new, (1, LANES))

    og = _sigmoid(o_ref[...].astype(F32))
    for h in range(M_HEADS):
        hh = outs[h]
        ms = jnp.mean(hh * hh, axis=1, keepdims=True)
        sl = slice(h * M_V, (h + 1) * M_V)
        y = hh * lax.rsqrt(ms + EPS) * go_ref[:, sl]
        y_ref[:, sl] = (y * og[:, sl]).astype(BF16)


MLSTM_NB, GLA_NB, SWA_NB = 1, 2, 2


def _mixer_specs(bsz, seq, nb):
    rows = lambda width, cb: pl.BlockSpec((nb, BLK, width), lambda b, c: (b, c, cb))
    full = lambda shp: pl.BlockSpec(shp, lambda b, c: (0,) * len(shp))
    return rows, full, (bsz // nb, seq // BLK)


def _mlstm(pm, ps, wc, bc, gb, go):
    bsz, seq, _ = pm.shape
    wd = M_HEADS * M_V
    MIX_NB = MLSTM_NB
    rows, full, grid = _mixer_specs(bsz, seq, MIX_NB)
    return pl.pallas_call(
        _per_batch(_mlstm_body, 4, 4),
        grid=grid,
        in_specs=[
            rows(wd, C_MQK // wd), rows(wd, C_MV // wd), rows(wd, C_MO // wd), rows(LANES, 0),
            full((CONV_W, 2 * M_HEADS * M_QK)), full((1, 2 * M_HEADS * M_QK)),
            full((1, LANES)), full((1, wd)),
        ],
        out_specs=rows(wd, 0),
        out_shape=jax.ShapeDtypeStruct((bsz, seq, wd), BF16),
        scratch_shapes=[
            pltpu.VMEM((MIX_NB, 8, 2 * M_HEADS * M_QK), F32),
            pltpu.VMEM((MIX_NB, M_HEADS, M_HEADS * M_QK, M_V), F32),
            pltpu.VMEM((MIX_NB, 8, M_HEADS * M_QK), F32),
            pltpu.VMEM((MIX_NB, 8, LANES), F32),
        ],
        compiler_params=_cparams(("arbitrary", "arbitrary")),
        name="mlstm",
    )(pm, pm, pm, ps, wc, bc, gb, go)


def _gla_body(qk_ref, v_ref, g_ref, sm_ref, wd_ref, bd_ref, go_ref, y_ref, s_ref, lhs_ref):
    L, C, SC = BLK, G_CHUNK, G_SUB
    nqk = G_HEADS * G_QK
    pair_k = 2 * G_QK
    pair_v = 2 * G_V
    n_pairs = G_HEADS // 2

    zdec = _dot(sm_ref[...].astype(BF16), wd_ref[...].astype(BF16)) + bd_ref[...]
    log_a = _log_sigmoid(zdec) / G_TAU
    row = lax.broadcasted_iota(I32, (L, L), 0)
    col = lax.broadcasted_iota(I32, (L, L), 1)
    same_chunk = (row // C) == (col // C)
    tri = ((row >= col) & same_chunk).astype(BF16)
    b = _split_dot(tri, log_a)
    bt = b.T

    q = qk_ref[:, :nqk].astype(F32) * (G_QK ** -0.5)
    k = qk_ref[:, nqk:].astype(F32)
    b_end = jnp.concatenate(
        [jnp.broadcast_to(b[(cc + 1) * C - 1:(cc + 1) * C, :], (C, nqk)) for cc in range(L // C)], axis=0)
    kdt = (k * jnp.exp(b_end - b)).T
    qe = q * jnp.exp(b)

    dloc = col - (row // SC) * SC
    diag_sel = jnp.where((dloc >= 0) & (dloc <= row % SC), dloc, -1)
    sub_r = (row % C) // SC
    off_sel = jnp.where(same_chunk & ((col % C) // SC < sub_r), sub_r, -1)
    diag_sel2 = jnp.concatenate([diag_sel, diag_sel], axis=1)
    off_sel2 = jnp.concatenate([off_sel, off_sel], axis=1)

    lane_s = lax.broadcasted_iota(I32, (pair_k, L), 1)
    head_k = lax.broadcasted_iota(I32, (L, pair_k), 1) // G_QK
    rblk = lax.broadcasted_iota(I32, (pair_k, pair_v), 0) // G_QK
    cblk = lax.broadcasted_iota(I32, (pair_k, pair_v), 1) // G_V
    blockdiag = rblk == cblk
    vrow = lax.broadcasted_iota(I32, (2 * L, pair_v), 0) // L
    vcol = lax.broadcasted_iota(I32, (2 * L, pair_v), 1) // G_V
    vdiag = vrow == vcol

    def stack_heads(x):
        return jnp.concatenate([jnp.where(head_k == hh, x, 0.0) for hh in range(2)], axis=0).astype(BF16)

    pair_out = []
    for p in range(n_pairs):
        kl = slice(p * pair_k, (p + 1) * pair_k)
        vl = slice(p * pair_v, (p + 1) * pair_v)
        bp, qp, kp = b[:, kl], q[:, kl], k[:, kl]
        v_p = v_ref[:, vl]
        b3 = bp.reshape(L // SC, SC, pair_k)

        def sub_row(i, b3=b3):
            return jnp.broadcast_to(b3[:, i:i + 1, :], b3.shape).reshape(L, pair_k)

        for i in range(SC):
            e = jnp.exp(jnp.minimum(bp - sub_row(i), 0.0))
            lhs_ref[i * L:(i + 1) * L, :] = (qp * e).astype(BF16)
        r = _dot_nt(lhs_ref[...], stack_heads(kp))
        att = jnp.zeros((L, 2 * L), F32)
        for i in range(SC):
            att = att + jnp.where(diag_sel2 == i, r[i * L:(i + 1) * L], 0.0)

        qs = (qp * jnp.exp(bp - sub_row(0))).astype(BF16)
        for sub in range(1, C // SC):
            bref = jnp.concatenate(
                [jnp.broadcast_to(bp[cc * C + sub * SC:cc * C + sub * SC + 1, :], (C, pair_k))
                 for cc in range(L // C)], axis=0)
            ks = kp * jnp.exp(jnp.minimum(bref - bp, 0.0))
            att = att + jnp.where(off_sel2 == sub, _dot_nt(qs, stack_heads(ks)), 0.0)

        vblk = jnp.where(vdiag, jnp.concatenate([v_p, v_p], axis=0), 0)
        intra = _dot(att.astype(BF16), vblk)

        outs = []
        for cc in range(L // C):
            r0 = cc * C
            s_p = s_ref[p]
            outs.append(_dot(qe[r0:r0 + C, kl].astype(BF16), s_p.astype(BF16)) + intra[r0:r0 + C])
            in_chunk = (lane_s >= r0) & (lane_s < r0 + C)
            kd = jnp.where(in_chunk, kdt[kl, :], 0.0).astype(BF16)
            upd = jnp.where(blockdiag, _dot(kd, v_p), 0.0)
            dcol = jnp.exp(bt[kl, r0 + C - 1:r0 + C])
            s_ref[p] = dcol * s_p + upd
        pair_out.append(outs)

    gate = g_ref[...].astype(F32)
    gate = gate * _sigmoid(gate)
    for p in range(n_pairs):
        o_p = jnp.concatenate(pair_out[p], axis=0)
        for hh in range(2):
            h = 2 * p + hh
            oh = o_p[:, hh * G_V:(hh + 1) * G_V]
            ms = jnp.mean(oh * oh, axis=1, keepdims=True)
            sl = slice(h * G_V, (h + 1) * G_V)
            y = oh * lax.rsqrt(ms + EPS) * go_ref[:, sl]
            y_ref[:, sl] = (y * gate[:, sl]).astype(BF16)


def _gla(pm, ps, wd, bd, go):
    bsz, seq, _ = pm.shape
    w = G_HEADS * G_V
    nqk = G_HEADS * G_QK
    MIX_NB = GLA_NB
    rows, full, grid = _mixer_specs(bsz, seq, MIX_NB)
    return pl.pallas_call(
        _per_batch(_gla_body, 4, 2),
        grid=grid,
        in_specs=[
            rows(w, C_GQK // w), rows(w, C_GV // w), rows(w, C_GG // w), rows(LANES, 0),
            full((LANES, nqk)), full((1, nqk)), full((1, w)),
        ],
        out_specs=rows(w, 0),
        out_shape=jax.ShapeDtypeStruct((bsz, seq, w), BF16),
        scratch_shapes=[
            pltpu.VMEM((MIX_NB, G_HEADS // 2, 2 * G_QK, 2 * G_V), F32),
            pltpu.VMEM((MIX_NB, G_SUB * BLK, 2 * G_QK), BF16),
        ],
        compiler_params=_cparams(("arbitrary", "arbitrary")),
        name="gla",
    )(pm, pm, pm, ps, wd, bd, go)


def _group_mean_sq(x, ind):
    return _split_dot_rhs(x * x, ind)


def _split_dot_rhs(a_f32, b_bf16):
    hi = a_f32.astype(BF16)
    lo = (a_f32 - hi.astype(F32)).astype(BF16)
    return _dot(hi, b_bf16) + _dot(lo, b_bf16)


def _swa_body(q_ref, kc_ref, kp_ref, vc_ref, vp_ref, bias_ref, gq_ref, gk_ref, sink_ref, ind_ref,
              y_ref):
    T = BLK
    heads_per_tile = LANES // A_HD
    n_tiles = A_HEADS // heads_per_tile
    grp = A_HEADS // A_KV
    ind = ind_ref[...]

    qn = []
    for t in range(n_tiles):
        sl = slice(t * LANES, (t + 1) * LANES)
        qt = q_ref[:, sl].astype(F32)
        ms = _group_mean_sq(qt, ind)
        qn.append((qt * lax.rsqrt(ms + EPS) * gq_ref[:, sl]).astype(BF16))
    kk = jnp.concatenate([kp_ref[...], kc_ref[...]], axis=0).astype(F32)
    kn = kk * lax.rsqrt(_group_mean_sq(kk, ind) + EPS) * gk_ref[...]
    vv = jnp.concatenate([vp_ref[...], vc_ref[...]], axis=0).astype(F32)
    k_sw = pltpu.roll(kn, A_HD, axis=1)
    v_sw = pltpu.roll(vv, A_HD, axis=1)
    half = lax.broadcasted_iota(I32, (2 * T, LANES), 1) // A_HD

    acc = [None] * n_tiles
    for g in range(A_KV):
        for p in range(heads_per_tile):
            ksrc, vsrc = (kn, vv) if g == p else (k_sw, v_sw)
            kz = jnp.where(half == p, ksrc, 0.0).astype(BF16)
            vz = jnp.where(half == p, vsrc, 0.0).astype(BF16)
            heads = [g * grp + heads_per_tile * u + p for u in range(grp // heads_per_tile)]
            qs = jnp.concatenate([qn[h // heads_per_tile] for h in heads], axis=0)
            sc_all = _dot_nt(qs, kz)
            probs, dens = [], []
            for u, h in enumerate(heads):
                sc = sc_all[u * T:(u + 1) * T] + bias_ref[h]
                sink = sink_ref[h]
                m = jnp.maximum(jnp.max(sc, axis=1, keepdims=True), sink)
                pr = jnp.exp2(sc - m)
                dens.append(jnp.sum(pr, axis=1, keepdims=True) + jnp.exp2(sink - m))
                probs.append(pr.astype(BF16))
            o_all = _dot(jnp.concatenate(probs, axis=0), vz)
            for u, h in enumerate(heads):
                o = o_all[u * T:(u + 1) * T] / dens[u]
                t = h // heads_per_tile
                acc[t] = o if acc[t] is None else acc[t] + o
    for t in range(n_tiles):
        y_ref[:, t * LANES:(t + 1) * LANES] = acc[t].astype(BF16)


def _swa(pm, bias, gq, gk, sinks, ind):
    bsz, seq, _ = pm.shape
    wq = A_HEADS * A_HD
    MIX_NB = SWA_NB
    rows, full, grid = _mixer_specs(bsz, seq, MIX_NB)
    cur = lambda cb: rows(LANES, cb)
    prv = lambda cb: pl.BlockSpec((MIX_NB, BLK, LANES), lambda b, c: (b, jnp.maximum(c - 1, 0), cb))
    return pl.pallas_call(
        _per_batch(_swa_body, 5, 0),
        grid=grid,
        in_specs=[
            rows(wq, C_AQ // wq),
            cur(C_AK // LANES), prv(C_AK // LANES), cur(C_AV // LANES), prv(C_AV // LANES),
            pl.BlockSpec((None, A_HEADS, BLK, 2 * BLK), lambda b, c: (jnp.minimum(c, 1), 0, 0, 0)),
            full((1, wq)), full((1, LANES)),
            pl.BlockSpec(memory_space=pltpu.SMEM),
            full((LANES, LANES)),
        ],
        out_specs=rows(wq, 0),
        out_shape=jax.ShapeDtypeStruct((bsz, seq, wq), BF16),
        compiler_params=_cparams(("arbitrary", "arbitrary")),
        name="swa",
    )(pm, pm, pm, pm, pm, bias, gq, gk, sinks, ind)


def _outproj_kernel(ym_ref, yg_ref, ya_ref, w_ref, x_ref, gt_ref, o_ref):
    y = jnp.concatenate([ym_ref[...], yg_ref[...], ya_ref[...]], axis=1)
    o_ref[...] = x_ref[...] + gt_ref[...] * _dot(y, w_ref[...])


def _outproj(ym, yg, ya, w, x2, mod4, seq):
    n, d = x2.shape
    tm, tn = 1024, d
    per_b = seq // tm
    return pl.pallas_call(
        _outproj_kernel,
        grid=(n // tm, d // tn),
        in_specs=[
            pl.BlockSpec((tm, ym.shape[1]), lambda i, j: (i, 0)),
            pl.BlockSpec((tm, yg.shape[1]), lambda i, j: (i, 0)),
            pl.BlockSpec((tm, ya.shape[1]), lambda i, j: (i, 0)),
            pl.BlockSpec((d, tn), lambda i, j: (0, j), pipeline_mode=pl.Buffered(1)),
            pl.BlockSpec((tm, tn), lambda i, j: (i, j)),
            pl.BlockSpec((None, None, 1, tn), lambda i, j: (i // per_b, 2, 0, j)),
        ],
        out_specs=pl.BlockSpec((tm, tn), lambda i, j: (i, j)),
        out_shape=jax.ShapeDtypeStruct((n, d), F32),
        compiler_params=_cparams(("arbitrary", "arbitrary")),
        name="out_proj",
    )(ym, yg, ya, w, x2, mod4)


def _swiglu_tile(h, wg_ref, wu_ref, wd_ref):
    gate = _dot(h, wg_ref[...].astype(BF16))
    up = _dot(h, wu_ref[...].astype(BF16))
    a = (gate * _sigmoid(gate) * up).astype(BF16)
    return _dot(a, wd_ref[...].astype(BF16))


def _ffn_kernel(x_ref, g_ref, sc_ref, sh_ref, gt_ref, wg_ref, wu_ref, wd_ref, o_ref, h_ref):
    f = pl.program_id(1)

    @pl.when(f == 0)
    def _():
        _norm_mod_rows(x_ref, g_ref, sc_ref, sh_ref, [h_ref])
        o_ref[...] = jnp.zeros_like(o_ref)

    o_ref[...] += _swiglu_tile(h_ref[...], wg_ref, wu_ref, wd_ref)

    @pl.when(f == pl.num_programs(1) - 1)
    def _():
        o_ref[...] = x_ref[...] + gt_ref[...] * o_ref[...]


def _ffn(x2, g, mod4, wg, wu, wd, seq):
    n, d = x2.shape
    dff = wg.shape[1]
    tm, tf = 1024, 512
    per_b = seq // tm
    mod_spec = lambda k: pl.BlockSpec((None, None, 1, d), lambda i, f: (i // per_b, k, 0, 0))
    return pl.pallas_call(
        _ffn_kernel,
        grid=(n // tm, dff // tf),
        in_specs=[
            pl.BlockSpec((tm, d), lambda i, f: (i, 0), pipeline_mode=pl.Buffered(1)),
            pl.BlockSpec((1, d), lambda i, f: (0, 0)),
            mod_spec(4), mod_spec(3), mod_spec(5),
            pl.BlockSpec((d, tf), lambda i, f: (0, f)),
            pl.BlockSpec((d, tf), lambda i, f: (0, f)),
            pl.BlockSpec((tf, d), lambda i, f: (f, 0)),
        ],
        out_specs=pl.BlockSpec((tm, d), lambda i, f: (i, 0), pipeline_mode=pl.Buffered(1)),
        out_shape=jax.ShapeDtypeStruct((n, d), F32),
        scratch_shapes=[pltpu.VMEM((tm, d), BF16)],
        compiler_params=_cparams(("arbitrary", "arbitrary")),
        name="ffn_dense",
    )(x2, g, mod4, mod4, mod4, wg, wu, wd)


def _pack_bf16_pairs(h):
    w = h.shape[1] // 2
    bits = lax.bitcast_convert_type(h.astype(BF16).astype(F32), jnp.uint32)
    return (bits[:, :w] >> 16) | (bits[:, w:] & jnp.uint32(0xFFFF0000))


def _unpack_bf16_pairs(u):
    lo = lax.bitcast_convert_type(u << 16, F32)
    hi = lax.bitcast_convert_type(u & jnp.uint32(0xFFFF0000), F32)
    return jnp.concatenate([lo, hi], axis=1).astype(BF16)


def _router_kernel(x_ref, g_ref, sc_ref, sh_ref, wh_ref, wl_ref, tri_ref, hp_ref, meta_ref, cnt_ref,
                   h_ref, run_ref):
    @pl.when(pl.program_id(0) == 0)
    def _():
        run_ref[...] = jnp.zeros_like(run_ref)

    g, sc, sh = g_ref[...], sc_ref[...], sh_ref[...]

    def norm_rows(r, carry):
        rows = pl.ds(pl.multiple_of(r * NORM_ROWS, NORM_ROWS), NORM_ROWS)
        hr = _norm_mod(x_ref[rows, :], g, sc, sh)
        h_ref[rows, :] = hr
        hp_ref[rows, :] = _pack_bf16_pairs(hr)
        return carry

    lax.fori_loop(0, x_ref.shape[0] // NORM_ROWS, norm_rows, 0, unroll=8)
    h = h_ref[...]
    h_hi = h.astype(BF16)
    h_lo = (h - h_hi.astype(F32)).astype(BF16)
    logits = _dot(h_hi, wh_ref[...]) + _dot(h_lo, wh_ref[...]) + _dot(h_hi, wl_ref[...])
    lane = lax.broadcasted_iota(I32, logits.shape, 1).astype(F32)
    l1 = jnp.where(lane < N_EXPERTS, logits, -jnp.inf)
    m1 = jnp.max(l1, axis=1, keepdims=True)
    i1 = jnp.min(jnp.where(l1 == m1, lane, float(LANES)), axis=1, keepdims=True)
    l2 = jnp.where(lane == i1, -jnp.inf, l1)
    m2 = jnp.max(l2, axis=1, keepdims=True)
    i2 = jnp.min(jnp.where(l2 == m2, lane, float(LANES)), axis=1, keepdims=True)
    e2 = jnp.exp(m2 - m1)
    w1 = 1.0 / (1.0 + e2)
    w2 = e2 / (1.0 + e2)
    member = jnp.where((lane == i1) | (lane == i2), 1.0, 0.0)
    run = run_ref[0:1, :]
    rank = _dot(tri_ref[...], member.astype(BF16)) + run
    p1 = jnp.sum(jnp.where(lane == i1, rank, 0.0), axis=1, keepdims=True)
    p2 = jnp.sum(jnp.where(lane == i2, rank, 0.0), axis=1, keepdims=True)
    run = run + jnp.sum(member, axis=0, keepdims=True)
    run_ref[0:1, :] = run
    cnt_ref[...] = jnp.broadcast_to(run, cnt_ref.shape)
    meta = jnp.zeros(logits.shape, F32)
    for idx, val in enumerate([i1, i2, w1, w2, p1, p2]):
        meta = jnp.where(lane == float(idx), val, meta)
    meta_ref[...] = meta


def _router(x2, g, mod4, wr_hi, wr_lo, seq):
    n, d = x2.shape
    tm = 512
    per_b = seq // tm
    tri = jnp.asarray(np.tril(np.ones((tm, tm), np.float32), -1), BF16)
    mod_spec = lambda k: pl.BlockSpec((None, None, 1, d), lambda i: (i // per_b, k, 0, 0))
    return pl.pallas_call(
        _router_kernel,
        grid=(n // tm,),
        in_specs=[
            pl.BlockSpec((tm, d), lambda i: (i, 0)),
            pl.BlockSpec((1, d), lambda i: (0, 0)),
            mod_spec(4), mod_spec(3),
            pl.BlockSpec((d, LANES), lambda i: (0, 0)),
            pl.BlockSpec((d, LANES), lambda i: (0, 0)),
            pl.BlockSpec((tm, tm), lambda i: (0, 0)),
        ],
        out_specs=[
            pl.BlockSpec((tm, d // 2), lambda i: (i, 0)),
            pl.BlockSpec((tm, LANES), lambda i: (i, 0)),
            pl.BlockSpec((8, LANES), lambda i: (0, 0)),
        ],
        out_shape=[jax.ShapeDtypeStruct((n, d // 2), jnp.uint32),
                   jax.ShapeDtypeStruct((n, LANES), F32),
                   jax.ShapeDtypeStruct((8, LANES), F32)],
        scratch_shapes=[pltpu.VMEM((tm, d), F32), pltpu.VMEM((8, LANES), F32)],
        compiler_params=_cparams(("arbitrary",)),
        name="router",
    )(x2, g, mod4, mod4, wr_hi, wr_lo, tri)


MOE_TM = 1024
DISP_TM = 512


def _dispatch_kernel(dest_hbm, hp_ref, xs_in_hbm, xs_hbm, idx_ref, sem_i, sem_r):
    del xs_in_hbm
    i = pl.program_id(0)
    n_idx = 2 * DISP_TM
    cp = pltpu.make_async_copy(dest_hbm.at[pl.ds(pl.multiple_of(i * n_idx, n_idx), n_idx)], idx_ref, sem_i)
    cp.start()
    cp.wait()

    def row_copy(r, k):
        return pltpu.make_async_copy(hp_ref.at[pl.ds(r, 1), :],
                                     xs_hbm.at[pl.ds(idx_ref[k * DISP_TM + r], 1), :], sem_r)

    def start(r, c):
        row_copy(r, 0).start()
        row_copy(r, 1).start()
        return c

    def wait(r, c):
        row_copy(r, 0).wait()
        row_copy(r, 1).wait()
        return c

    lax.fori_loop(0, DISP_TM, start, 0, unroll=8)
    lax.fori_loop(0, DISP_TM, wait, 0, unroll=8)


def _dispatch(dest_tiles, hp, n_rows):
    n, w = hp.shape
    xs0 = jnp.zeros((n_rows, w), jnp.uint32)
    return pl.pallas_call(
        _dispatch_kernel,
        grid=(n // DISP_TM,),
        in_specs=[
            pl.BlockSpec(memory_space=pl.ANY),
            pl.BlockSpec((DISP_TM, w), lambda i: (i, 0)),
            pl.BlockSpec(memory_space=pl.ANY),
        ],
        out_specs=pl.BlockSpec(memory_space=pl.ANY),
        out_shape=jax.ShapeDtypeStruct((n_rows, w), jnp.uint32),
        scratch_shapes=[
            pltpu.SMEM((2 * DISP_TM,), I32),
            pltpu.SemaphoreType.DMA(()),
            pltpu.SemaphoreType.DMA(()),
        ],
        input_output_aliases={2: 0},
        compiler_params=_cparams(("arbitrary",)),
        name="moe_dispatch",
    )(dest_tiles, hp, xs0)


MOE_SUB = 256


def _moe_kernel(te_ref, nu_ref, tr_ref, xs_ref, wg_ref, wu_ref, wd_ref, o_ref, xb_ref):
    t = pl.program_id(0)
    f = pl.program_id(1)
    rows = tr_ref[t]

    @pl.when(f == 0)
    def _():
        o_ref[...] = jnp.zeros_like(o_ref)

    @pl.when((rows > 0) & (f == 0))
    def _():
        xb_ref[...] = _unpack_bf16_pairs(xs_ref[...])

    @pl.when(rows == MOE_TM)
    def _():
        o_ref[...] += _swiglu_tile(xb_ref[...], wg_ref, wu_ref, wd_ref)

    for sb in range(MOE_TM // MOE_SUB):
        @pl.when((rows < MOE_TM) & (rows > sb * MOE_SUB))
        def _(sb=sb):
            sl = slice(sb * MOE_SUB, (sb + 1) * MOE_SUB)
            o_ref[sl, :] += _swiglu_tile(xb_ref[sl, :], wg_ref, wu_ref, wd_ref)


def _moe(tile_expert, n_used, tile_rows, xs, wg, wu, wd):
    d = wg.shape[1]
    n_tiles = tile_expert.shape[0]
    dff = wg.shape[2]
    tf = 512
    nf = dff // tf

    def f_blk(t, f, nu):
        return jnp.where(t < nu[0], f, nf - 1)

    grid_spec = pltpu.PrefetchScalarGridSpec(
        num_scalar_prefetch=3,
        grid=(n_tiles, nf),
        in_specs=[
            pl.BlockSpec((MOE_TM, d // 2), lambda t, f, te, nu, tr: (jnp.minimum(t, nu[0] - 1), 0)),
            pl.BlockSpec((None, d, tf), lambda t, f, te, nu, tr: (te[t], 0, f_blk(t, f, nu))),
            pl.BlockSpec((None, d, tf), lambda t, f, te, nu, tr: (te[t], 0, f_blk(t, f, nu))),
            pl.BlockSpec((None, tf, d), lambda t, f, te, nu, tr: (te[t], f_blk(t, f, nu), 0)),
        ],
        out_specs=pl.BlockSpec((MOE_TM, d), lambda t, f, te, nu, tr: (t, 0), pipeline_mode=pl.Buffered(1)),
        scratch_shapes=[pltpu.VMEM((MOE_TM, d), BF16)],
    )
    return pl.pallas_call(
        _moe_kernel,
        grid_spec=grid_spec,
        out_shape=jax.ShapeDtypeStruct((n_tiles * MOE_TM, d), F32),
        compiler_params=_cparams(("arbitrary", "arbitrary")),
        name="moe_experts",
    )(tile_expert, n_used, tile_rows, xs, wg, wu, wd)


COMB_TM = 512


def _combine_kernel(dest_hbm, y_hbm, x_ref, meta_ref, gt_ref, o_ref, idx_ref, y_ref, sem_i, sem_r):
    i = pl.program_id(0)
    n_idx = 2 * COMB_TM

    def row_copy(slot, j):
        return pltpu.make_async_copy(
            y_hbm.at[pl.ds(idx_ref[slot * n_idx + j], 1), :],
            y_ref.at[slot, pl.ds(j, 1), :], sem_r.at[slot])

    def fetch(step, slot):
        cp = pltpu.make_async_copy(
            dest_hbm.at[pl.ds(pl.multiple_of(step * n_idx, n_idx), n_idx)],
            idx_ref.at[pl.ds(pl.multiple_of(slot * n_idx, n_idx), n_idx)], sem_i)
        cp.start()
        cp.wait()

        def start(j, c):
            row_copy(slot, j).start()
            return c

        lax.fori_loop(0, n_idx, start, 0, unroll=8)

    @pl.when(i == 0)
    def _():
        fetch(i, 0)

    @pl.when(i + 1 < pl.num_programs(0))
    def _():
        fetch(i + 1, (i + 1) % 2)

    slot = i % 2

    def wait(j, c):
        row_copy(slot, j).wait()
        return c

    lax.fori_loop(0, n_idx, wait, 0, unroll=8)
    w1 = meta_ref[:, 2:3]
    w2 = meta_ref[:, 3:4]
    y1 = y_ref[slot, 0:COMB_TM, :]
    y2 = y_ref[slot, COMB_TM:n_idx, :]
    o_ref[...] = x_ref[...] + gt_ref[...] * (w1 * y1 + w2 * y2)


def _combine(dest, y_sorted, x2, meta, mod4, seq):
    n, d = x2.shape
    tm = COMB_TM
    per_b = seq // tm
    return pl.pallas_call(
        _combine_kernel,
        grid=(n // tm,),
        in_specs=[
            pl.BlockSpec(memory_space=pl.ANY),
            pl.BlockSpec(memory_space=pl.ANY),
            pl.BlockSpec((tm, d), lambda i: (i, 0)),
            pl.BlockSpec((tm, LANES), lambda i: (i, 0)),
            pl.BlockSpec((None, None, 1, d), lambda i: (i // per_b, 5, 0, 0)),
        ],
        out_specs=pl.BlockSpec((tm, d), lambda i: (i, 0)),
        out_shape=jax.ShapeDtypeStruct((n, d), F32),
        scratch_shapes=[
            pltpu.SMEM((2 * 2 * tm,), I32),
            pltpu.VMEM((2, 2 * tm, d), F32),
            pltpu.SemaphoreType.DMA(()),
            pltpu.SemaphoreType.DMA((2,)),
        ],
        compiler_params=_cparams(("arbitrary",)),
        name="moe_combine",
    )(dest, y_sorted, x2, meta, mod4)


def _moe_layer(x2, g, mod4, w_router, wg, wu, wd, seq):
    n, d = x2.shape
    wr = jnp.zeros((d, LANES), F32).at[:, :N_EXPERTS].set(w_router)
    wr_hi = wr.astype(BF16)
    wr_lo = (wr - wr_hi.astype(F32)).astype(BF16)
    hp, meta, cnt = _router(x2, g, mod4, wr_hi, wr_lo, seq)

    n_tiles = 2 * n // MOE_TM + N_EXPERTS
    counts = cnt[0, :N_EXPERTS].astype(I32)
    tiles_per = (counts + MOE_TM - 1) // MOE_TM
    tile_end = jnp.cumsum(tiles_per)
    row_start = (tile_end - tiles_per) * MOE_TM
    n_used = tile_end[-1]
    idx = meta[:, 0:2].astype(I32)
    pos = meta[:, 4:6].astype(I32)
    start_of = sum(jnp.where(idx == e, row_start[e], 0) for e in range(N_EXPERTS))
    dest = start_of + pos
    all_tiles = jnp.arange(n_tiles, dtype=I32)
    tile_ids = jnp.minimum(all_tiles, n_used - 1)
    tile_expert = jnp.sum(tile_ids[:, None] >= tile_end[None, :], axis=1).astype(I32)
    onehot_e = tile_expert[:, None] == jnp.arange(N_EXPERTS, dtype=I32)[None, :]
    first_tile = jnp.sum(jnp.where(onehot_e, (tile_end - tiles_per)[None, :], 0), axis=1)
    cnt_tile = jnp.sum(jnp.where(onehot_e, counts[None, :], 0), axis=1)
    tile_rows = jnp.clip(cnt_tile - (all_tiles - first_tile) * MOE_TM, 0, MOE_TM)
    tile_rows = jnp.where(all_tiles < n_used, tile_rows, 0).astype(I32)
    assert COMB_TM == DISP_TM
    dest_tiles = dest.reshape(n // COMB_TM, COMB_TM, 2).transpose(0, 2, 1).reshape(-1)

    xs = _dispatch(dest_tiles, hp, n_tiles * MOE_TM)
    y_sorted = _moe(tile_expert, n_used.reshape(1).astype(I32), tile_rows, xs, wg, wu, wd)
    return _combine(dest_tiles, y_sorted, x2, meta, mod4, seq)


def _t5_bucket(dist):
    max_exact = N_BUCKETS // 2
    d = np.maximum(dist, 1).astype(np.float32)
    large = max_exact + (np.log(d / max_exact) / np.log(MAX_DIST / max_exact)
                         * (N_BUCKETS - max_exact)).astype(np.int32)
    large = np.minimum(large, N_BUCKETS - 1)
    return np.where(dist < max_exact, dist, large).astype(np.int32)


def _pack_w_in(w):
    splits = [M_HEADS * M_QK, M_HEADS * M_QK, M_HEADS * M_V, M_HEADS * M_V, 2 * M_HEADS,
              G_HEADS * G_QK, G_HEADS * G_QK, G_HEADS * G_V, G_HEADS * G_V, G_RANK,
              A_HEADS * A_HD, A_KV * A_HD, A_KV * A_HD]
    offs = np.concatenate([[0], np.cumsum(splits)])
    seg = lambda i: w[:, offs[i]:offs[i + 1]]
    main = jnp.concatenate([seg(i) for i in (0, 1, 2, 3, 5, 6, 7, 8, 10, 11, 12)], axis=1)
    small = jnp.concatenate(
        [seg(4), seg(9), jnp.zeros((w.shape[0], LANES - 2 * M_HEADS - G_RANK), w.dtype)], axis=1)
    return main.astype(BF16), small.astype(BF16)


def kernel(x, c, w_ada, b_ada, g_mix_norm, g_ffn_norm, w_in, b_gates_m, w_conv_m, b_conv_m, g_out_m,
           w_gla_decay, b_gla_decay, g_out_g, g_qnorm, g_knorm, sinks, rel_bias, w_out, w_ffn_gate,
           w_ffn_up, w_ffn_down, w_router, w_moe_gate, w_moe_up, w_moe_down):
    bsz, seq, d = x.shape
    depth = w_ada.shape[0]
    n = bsz * seq
    assert seq % 1024 == 0 and d == 2048 and n % MOE_TM == 0 and bsz % max(MLSTM_NB, GLA_NB, SWA_NB) == 0

    mod = _ada(c, w_ada, b_ada)

    jj = np.arange(BLK)[:, None]
    ss = np.arange(2 * BLK)[None, :]
    buckets = _t5_bucket(np.clip(jj + BLK - ss, 0, None))
    onehot = jnp.asarray(np.eye(N_BUCKETS, dtype=np.float32)[buckets.reshape(-1)])
    bias = jnp.einsum("pb,bh->hp", onehot, rel_bias.astype(F32),
                      precision=lax.Precision.HIGHEST).reshape(A_HEADS, BLK, 2 * BLK)
    in_window = (jj + BLK - ss >= 0) & (jj + BLK - ss < WINDOW)
    masks = np.stack([in_window & (ss >= BLK), in_window])
    bias = jnp.where(jnp.asarray(masks)[:, None], (bias * LOG2E)[None], -jnp.inf)
    ind = jnp.asarray(np.kron(np.eye(LANES // A_HD), np.ones((A_HD, A_HD))) / A_HD, BF16)

    x2 = x.reshape(n, d)
    for l in range(depth):
        mod4 = mod[l].reshape(bsz, 6, 1, d)
        w_main, w_small = _pack_w_in(w_in[l])
        pm, ps = _inproj(x2, g_mix_norm[l].reshape(1, d), mod4, w_main, w_small, seq)
        pm = pm.reshape(bsz, seq, -1)
        ps = ps.reshape(bsz, seq, -1)

        gb = jnp.zeros((1, LANES), F32).at[0, :2 * M_HEADS].set(b_gates_m[l].reshape(-1))
        ym = _mlstm(pm, ps, w_conv_m[l], b_conv_m[l].reshape(1, -1), gb, g_out_m[l].reshape(1, -1))
        wdec = jnp.zeros((LANES, G_HEADS * G_QK), F32).at[S_GA:S_GA + G_RANK].set(w_gla_decay[l])
        yg = _gla(pm, ps, wdec, b_gla_decay[l].reshape(1, -1), g_out_g[l].reshape(1, -1))
        gq = jnp.tile(g_qnorm[l], A_HEADS).reshape(1, -1) * (A_HD ** -0.5 * LOG2E)
        gk = jnp.tile(g_knorm[l], A_KV).reshape(1, -1)
        ya = _swa(pm, bias, gq, gk, sinks[l] * LOG2E, ind)
        x2 = _outproj(ym.reshape(n, -1), yg.reshape(n, -1), ya.reshape(n, -1), w_out[l].astype(BF16),
                      x2, mod4, seq)

        gf = g_ffn_norm[l].reshape(1, d)
        if l % 2 == 0:
            j = l // 2
            x2 = _ffn(x2, gf, mod4, w_ffn_gate[j], w_ffn_up[j], w_ffn_down[j], seq)
        else:
            j = l // 2
            x2 = _moe_layer(x2, gf, mod4, w_router[j], w_moe_gate[j], w_moe_up[j], w_moe_down[j], seq)
    return x2.reshape(bsz, seq, d)
```

```python
import functools

import numpy as np
import jax
import jax.numpy as jnp
from jax import lax
from jax.experimental import pallas as pl
from jax.experimental.pallas import tpu as pltpu

F32 = jnp.float32
BF16 = jnp.bfloat16
I32 = jnp.int32

M_HEADS = 4
M_QK = 64
M_V = 128
CONV_W = 4
G_HEADS = 4
G_QK = 64
G_V = 128
G_RANK = 16
G_TAU = 16.0
G_CHUNK = 64
G_SUB = 16
A_HD = 64
A_HEADS = 16
A_KV = 2
WINDOW = 128
N_BUCKETS = 32
MAX_DIST = 128
N_EXPERTS = 8
EPS = 1e-6
LOG2E = 1.4426950408889634

LANES = 128
BLK = 128
VMEM_LIMIT = 56 * 1024 * 1024

C_MQK, C_MV, C_MO = 0, 512, 1024
C_GQK, C_GV, C_GG = 1536, 2048, 2560
C_AQ, C_AK, C_AV = 3072, 4096, 4224
N_MAIN = 4352
S_GA = 8


def _cparams(sem):
    return pltpu.CompilerParams(dimension_semantics=sem, vmem_limit_bytes=VMEM_LIMIT)


def _dot(a, b):
    return jnp.dot(a, b, preferred_element_type=F32)


def _dot_nt(a, b):
    return lax.dot_general(a, b, (((1,), (1,)), ((), ())), preferred_element_type=F32)


def _split_dot(a, b_f32):
    hi = b_f32.astype(BF16)
    lo = (b_f32 - hi.astype(F32)).astype(BF16)
    return _dot(a, hi) + _dot(a, lo)


def _sigmoid(x):
    return 1.0 / (1.0 + jnp.exp(-x))


def _log_sigmoid(x):
    return jnp.minimum(x, 0.0) - jnp.log(1.0 + jnp.exp(-jnp.abs(x)))


def _norm_mod(x, g, sc, sh):
    ms = jnp.mean(x * x, axis=-1, keepdims=True)
    return (x * lax.rsqrt(ms + EPS) * g) * (1.0 + sc) + sh


NORM_ROWS = 16


def _norm_mod_rows(x_ref, g_ref, sc_ref, sh_ref, out_refs):
    g, sc, sh = g_ref[...], sc_ref[...], sh_ref[...]

    def body(r, carry):
        rows = pl.ds(pl.multiple_of(r * NORM_ROWS, NORM_ROWS), NORM_ROWS)
        h = _norm_mod(x_ref[rows, :], g, sc, sh)
        for ref in out_refs:
            ref[rows, :] = h.astype(ref.dtype)
        return carry

    lax.fori_loop(0, x_ref.shape[0] // NORM_ROWS, body, 0, unroll=8)


def _ada_kernel(c_ref, w_ref, b_ref, o_ref):
    c = c_ref[...]
    cond = (c * _sigmoid(c)).astype(BF16)
    o_ref[...] = _dot(cond, w_ref[...].astype(BF16)) + b_ref[...]


def _ada(c, w_ada, b_ada):
    depth, d, n6 = w_ada.shape
    bsz = c.shape[0]
    tn = 1024
    return pl.pallas_call(
        _ada_kernel,
        grid=(depth, n6 // tn),
        in_specs=[
            pl.BlockSpec((bsz, d), lambda l, n: (0, 0)),
            pl.BlockSpec((None, d, tn), lambda l, n: (l, 0, n)),
            pl.BlockSpec((None, 1, tn), lambda l, n: (l, 0, n)),
        ],
        out_specs=pl.BlockSpec((None, bsz, tn), lambda l, n: (l, 0, n)),
        out_shape=jax.ShapeDtypeStruct((depth, bsz, n6), F32),
        compiler_params=_cparams(("arbitrary", "arbitrary")),
        name="ada_mod",
    )(c, w_ada, b_ada.reshape(depth, 1, n6))


def _inproj_kernel(x_ref, g_ref, sc_ref, sh_ref, wm_ref, ws_ref, om_ref, os_ref, h_ref):
    @pl.when(pl.program_id(1) == 0)
    def _():
        _norm_mod_rows(x_ref, g_ref, sc_ref, sh_ref, [h_ref])
        os_ref[...] = _dot(h_ref[...], ws_ref[...])

    om_ref[...] = _dot(h_ref[...], wm_ref[...]).astype(BF16)


def _inproj(x2, g, mod4, w_main, w_small, seq):
    n, d = x2.shape
    tm, tn = 1024, N_MAIN // 2
    per_b = seq // tm
    mod_spec = lambda k: pl.BlockSpec((None, None, 1, d), lambda i, j: (i // per_b, k, 0, 0))
    return pl.pallas_call(
        _inproj_kernel,
        grid=(n // tm, N_MAIN // tn),
        in_specs=[
            pl.BlockSpec((tm, d), lambda i, j: (i, 0)),
            pl.BlockSpec((1, d), lambda i, j: (0, 0)),
            mod_spec(1), mod_spec(0),
            pl.BlockSpec((d, tn), lambda i, j: (0, j)),
            pl.BlockSpec((d, LANES), lambda i, j: (0, 0)),
        ],
        out_specs=[
            pl.BlockSpec((tm, tn), lambda i, j: (i, j)),
            pl.BlockSpec((tm, LANES), lambda i, j: (i, 0)),
        ],
        out_shape=[jax.ShapeDtypeStruct((n, N_MAIN), BF16),
                   jax.ShapeDtypeStruct((n, LANES), F32)],
        scratch_shapes=[pltpu.VMEM((tm, d), BF16)],
        compiler_params=_cparams(("arbitrary", "arbitrary")),
        name="in_proj",
    )(x2, g, mod4, mod4, w_main, w_small)


def _per_batch(body, n_batch_refs, n_state_refs):
    def kern(*refs):
        n_shared = len(refs) - n_batch_refs - n_state_refs - 1
        batch_in, shared = refs[:n_batch_refs], refs[n_batch_refs:n_batch_refs + n_shared]
        y_ref = refs[n_batch_refs + n_shared]
        state = refs[len(refs) - n_state_refs:]

        @pl.when(pl.program_id(1) == 0)
        def _():
            for st in state:
                st[...] = jnp.zeros_like(st)

        for bb in range(y_ref.shape[0]):
            body(*[r.at[bb] for r in batch_in], *shared, y_ref.at[bb], *[st.at[bb] for st in state])
    return kern


def _mlstm_body(qk_ref, v_ref, o_ref, sm_ref, wc_ref, bc_ref, gb_ref, go_ref, y_ref,
                tail_ref, ct_ref, n_ref, m_ref):
    L = BLK
    nqk = M_HEADS * M_QK

    x = qk_ref[...].astype(F32)
    prev = tail_ref[...]
    row8 = lax.broadcasted_iota(I32, prev.shape, 0)
    acc = x * wc_ref[CONV_W - 1:CONV_W, :] + bc_ref[...]
    for dlt in range(1, CONV_W):
        xs = pltpu.roll(x, dlt, axis=0)
        ps = pltpu.roll(prev, dlt, axis=0)
        top = jnp.where(row8 < dlt, ps, xs[0:8])
        xs = jnp.concatenate([top, xs[8:]], axis=0)
        acc = acc + xs * wc_ref[CONV_W - 1 - dlt:CONV_W - dlt, :]
    tail_ref[...] = x[L - 8:L]
    qk = acc * _sigmoid(acc)
    q = qk[:, :nqk]
    k = qk[:, nqk:] * (M_QK ** -0.5)
    kt_b = k.T.astype(BF16)

    gates = sm_ref[...] + gb_ref[...]
    row = lax.broadcasted_iota(I32, (L, L), 0)
    col = lax.broadcasted_iota(I32, (L, L), 1)
    causal = row >= col
    tri = causal.astype(BF16)
    bcum = _split_dot(tri, _log_sigmoid(gates))
    z = jnp.where(col < M_HEADS, gates, bcum)
    zt = z.T
    lane_q = lax.broadcasted_iota(I32, (L, nqk), 1)

    outs = []
    for h in range(M_HEADS):
        in_head = (lane_q >= h * M_QK) & (lane_q < (h + 1) * M_QK)
        qm = jnp.where(in_head, q, 0.0)
        qm_b = qm.astype(BF16)
        v_h = v_ref[:, h * M_V:(h + 1) * M_V]
        m_prev = m_ref[h:h + 1, 0:1]
        n_h = n_ref[h:h + 1, :]
        ct_h = ct_ref[h]
        bcol = z[:, M_HEADS + h:M_HEADS + h + 1]
        icol = z[:, h:h + 1]
        brow = zt[M_HEADS + h:M_HEADS + h + 1, :]
        irow = zt[h:h + 1, :]
        logd = jnp.where(causal, bcol - brow + irow, -jnp.inf)
        log_inter = bcol + m_prev
        m_row = jnp.maximum(log_inter, jnp.max(logd, axis=1, keepdims=True))
        dmat = jnp.exp(logd - m_row)
        w_inter = jnp.exp(log_inter - m_row)
        s = _dot(qm_b, kt_b) * dmat
        num = w_inter * _dot(qm_b, ct_h.astype(BF16)) + _dot(s.astype(BF16), v_h)
        den = (w_inter * jnp.sum(qm * n_h, axis=1, keepdims=True)
               + jnp.sum(s, axis=1, keepdims=True))
        outs.append(num / jnp.maximum(jnp.abs(den), jnp.exp(-m_row)))
        b_last = bcol[L - 1:L, :]
        log_w = b_last - bcol + icol
        m_new = jnp.maximum(b_last + m_prev, jnp.max(log_w, axis=0, keepdims=True))
        w = jnp.exp(log_w - m_new)
        decay = jnp.exp(b_last + m_prev - m_new)
        vw = (v_h.astype(F32) * w).astype(BF16)
        ct_ref[h] = decay * ct_h + _dot(kt_b, vw)
        n_ref[h:h + 1, :] = decay * n_h + jnp.sum(k * w, axis=0, keepdims=True)
        m_ref[h:h + 1, :] = jnp.broadcast_to(m_new, (1, LANES))

    og = _sigmoid(o_ref[...].astype(F32))
    for h in range(M_HEADS):
        hh = outs[h]
        ms = jnp.mean(hh * hh, axis=1, keepdims=True)
        sl = slice(h * M_V, (h + 1) * M_V)
        y = hh * lax.rsqrt(ms + EPS) * go_ref[:, sl]
        y_ref[:, sl] = (y * og[:, sl]).astype(BF16)


MLSTM_NB, GLA_NB, SWA_NB = 1, 2, 2


def _mixer_specs(bsz, seq, nb):
    rows = lambda width, cb: pl.BlockSpec((nb, BLK, width), lambda b, c: (b, c, cb))
    full = lambda shp: pl.BlockSpec(shp, lambda b, c: (0,) * len(shp))
    return rows, full, (bsz // nb, seq // BLK)


def _mlstm(pm, ps, wc, bc, gb, go):
    bsz, seq, _ = pm.shape
    wd = M_HEADS * M_V
    MIX_NB = MLSTM_NB
    rows, full, grid = _mixer_specs(bsz, seq, MIX_NB)
    return pl.pallas_call(
        _per_batch(_mlstm_body, 4, 4),
        grid=grid,
        in_specs=[
            rows(wd, C_MQK // wd), rows(wd, C_MV // wd), rows(wd, C_MO // wd), rows(LANES, 0),
            full((CONV_W, 2 * M_HEADS * M_QK)), full((1, 2 * M_HEADS * M_QK)),
            full((1, LANES)), full((1, wd)),
        ],
        out_specs=rows(wd, 0),
        out_shape=jax.ShapeDtypeStruct((bsz, seq, wd), BF16),
        scratch_shapes=[
            pltpu.VMEM((MIX_NB, 8, 2 * M_HEADS * M_QK), F32),
            pltpu.VMEM((MIX_NB, M_HEADS, M_HEADS * M_QK, M_V), F32),
            pltpu.VMEM((MIX_NB, 8, M_HEADS * M_QK), F32),
            pltpu.VMEM((MIX_NB, 8, LANES), F32),
        ],
        compiler_params=_cparams(("arbitrary", "arbitrary")),
        name="mlstm",
    )(pm, pm, pm, ps, wc, bc, gb, go)


def _gla_body(qk_ref, v_ref, g_ref, sm_ref, wd_ref, bd_ref, go_ref, y_ref, s_ref):
    L, C, SC = BLK, G_CHUNK, G_SUB
    nqk = G_HEADS * G_QK
    pair_k = 2 * G_QK
    pair_v = 2 * G_V
    n_pairs = G_HEADS // 2

    zdec = _dot(sm_ref[...].astype(BF16), wd_ref[...].astype(BF16)) + bd_ref[...]
    log_a = _log_sigmoid(zdec) / G_TAU
    row = lax.broadcasted_iota(I32, (L, L), 0)
    col = lax.broadcasted_iota(I32, (L, L), 1)
    same_chunk = (row // C) == (col // C)
    tri = ((row >= col) & same_chunk).astype(BF16)
    b = _split_dot(tri, log_a)
    bt = b.T

    q = qk_ref[:, :nqk].astype(F32) * (G_QK ** -0.5)
    k = qk_ref[:, nqk:].astype(F32)
    b_end = jnp.concatenate(
        [jnp.broadcast_to(b[(cc + 1) * C - 1:(cc + 1) * C, :], (C, nqk)) for cc in range(L // C)], axis=0)
    kdt = (k * jnp.exp(b_end - b)).T
    qe = q * jnp.exp(b)

    dloc = col - (row // SC) * SC
    diag_sel = jnp.where((dloc >= 0) & (dloc <= row % SC), dloc, -1)
    sub_r = (row % C) // SC
    off_sel = jnp.where(same_chunk & ((col % C) // SC < sub_r), sub_r, -1)
    diag_sel2 = jnp.concatenate([diag_sel, diag_sel], axis=1)
    off_sel2 = jnp.concatenate([off_sel, off_sel], axis=1)

    lane_s = lax.broadcasted_iota(I32, (pair_k, L), 1)
    head_k = lax.broadcasted_iota(I32, (L, pair_k), 1) // G_QK
    rblk = lax.broadcasted_iota(I32, (pair_k, pair_v), 0) // G_QK
    cblk = lax.broadcasted_iota(I32, (pair_k, pair_v), 1) // G_V
    blockdiag = rblk == cblk
    vrow = lax.broadcasted_iota(I32, (2 * L, pair_v), 0) // L
    vcol = lax.broadcasted_iota(I32, (2 * L, pair_v), 1) // G_V
    vdiag = vrow == vcol

    def stack_heads(x):
        return jnp.concatenate([jnp.where(head_k == hh, x, 0.0) for hh in range(2)], axis=0).astype(BF16)

    pair_out = []
    for p in range(n_pairs):
        kl = slice(p * pair_k, (p + 1) * pair_k)
        vl = slice(p * pair_v, (p + 1) * pair_v)
        bp, qp, kp = b[:, kl], q[:, kl], k[:, kl]
        v_p = v_ref[:, vl]
        b3 = bp.reshape(L // SC, SC, pair_k)

        def sub_row(i, b3=b3):
            return jnp.broadcast_to(b3[:, i:i + 1, :], b3.shape).reshape(L, pair_k)

        kst = stack_heads(kp)
        att = jnp.zeros((L, 2 * L), F32)
        for i in range(SC):
            e = jnp.exp(jnp.minimum(bp - sub_row(i), 0.0))
            r = _dot_nt((qp * e).astype(BF16), kst)
            att = att + jnp.where(diag_sel2 == i, r, 0.0)

        qs = (qp * jnp.exp(bp - sub_row(0))).astype(BF16)
        for sub in range(1, C // SC):
            bref = jnp.concatenate(
                [jnp.broadcast_to(bp[cc * C + sub * SC:cc * C + sub * SC + 1, :], (C, pair_k))
                 for cc in range(L // C)], axis=0)
            ks = kp * jnp.exp(jnp.minimum(bref - bp, 0.0))
            att = att + jnp.where(off_sel2 == sub, _dot_nt(qs, stack_heads(ks)), 0.0)

        vblk = jnp.where(vdiag, jnp.concatenate([v_p, v_p], axis=0), 0)
        intra = _dot(att.astype(BF16), vblk)

        outs = []
        for cc in range(L // C):
            r0 = cc * C
            s_p = s_ref[p]
            outs.append(_dot(qe[r0:r0 + C, kl].astype(BF16), s_p.astype(BF16)) + intra[r0:r0 + C])
            in_chunk = (lane_s >= r0) & (lane_s < r0 + C)
            kd = jnp.where(in_chunk, kdt[kl, :], 0.0).astype(BF16)
            upd = jnp.where(blockdiag, _dot(kd, v_p), 0.0)
            dcol = jnp.exp(bt[kl, r0 + C - 1:r0 + C])
            s_ref[p] = dcol * s_p + upd
        pair_out.append(outs)

    gate = g_ref[...].astype(F32)
    gate = gate * _sigmoid(gate)
    for p in range(n_pairs):
        o_p = jnp.concatenate(pair_out[p], axis=0)
        for hh in range(2):
            h = 2 * p + hh
            oh = o_p[:, hh * G_V:(hh + 1) * G_V]
            ms = jnp.mean(oh * oh, axis=1, keepdims=True)
            sl = slice(h * G_V, (h + 1) * G_V)
            y = oh * lax.rsqrt(ms + EPS) * go_ref[:, sl]
            y_ref[:, sl] = (y * gate[:, sl]).astype(BF16)


def _gla(pm, ps, wd, bd, go):
    bsz, seq, _ = pm.shape
    w = G_HEADS * G_V
    nqk = G_HEADS * G_QK
    MIX_NB = GLA_NB
    rows, full, grid = _mixer_specs(bsz, seq, MIX_NB)
    return pl.pallas_call(
        _per_batch(_gla_body, 4, 1),
        grid=grid,
        in_specs=[
            rows(w, C_GQK // w), rows(w, C_GV // w), rows(w, C_GG // w), rows(LANES, 0),
            full((LANES, nqk)), full((1, nqk)), full((1, w)),
        ],
        out_specs=rows(w, 0),
        out_shape=jax.ShapeDtypeStruct((bsz, seq, w), BF16),
        scratch_shapes=[
            pltpu.VMEM((MIX_NB, G_HEADS // 2, 2 * G_QK, 2 * G_V), F32),
        ],
        compiler_params=_cparams(("arbitrary", "arbitrary")),
        name="gla",
    )(pm, pm, pm, ps, wd, bd, go)


def _group_mean_sq(x, ind):
    return _split_dot_rhs(x * x, ind)


def _split_dot_rhs(a_f32, b_bf16):
    hi = a_f32.astype(BF16)
    lo = (a_f32 - hi.astype(F32)).astype(BF16)
    return _dot(hi, b_bf16) + _dot(lo, b_bf16)


def _swa_body(q_ref, kc_ref, kp_ref, vc_ref, vp_ref, bias_ref, gq_ref, gk_ref, sink_ref, ind_ref,
              y_ref):
    T = BLK
    heads_per_tile = LANES // A_HD
    n_tiles = A_HEADS // heads_per_tile
    grp = A_HEADS // A_KV
    ind = ind_ref[...]

    qn = []
    for t in range(n_tiles):
        sl = slice(t * LANES, (t + 1) * LANES)
        qt = q_ref[:, sl].astype(F32)
        ms = _group_mean_sq(qt, ind)
        qn.append((qt * lax.rsqrt(ms + EPS) * gq_ref[:, sl]).astype(BF16))
    kk = jnp.concatenate([kp_ref[...], kc_ref[...]], axis=0).astype(F32)
    kn = kk * lax.rsqrt(_group_mean_sq(kk, ind) + EPS) * gk_ref[...]
    vv = jnp.concatenate([vp_ref[...], vc_ref[...]], axis=0).astype(F32)
    k_sw = pltpu.roll(kn, A_HD, axis=1)
    v_sw = pltpu.roll(vv, A_HD, axis=1)
    half = lax.broadcasted_iota(I32, (2 * T, LANES), 1) // A_HD

    acc = [None] * n_tiles
    for g in range(A_KV):
        for p in range(heads_per_tile):
            ksrc, vsrc = (kn, vv) if g == p else (k_sw, v_sw)
            kz = jnp.where(half == p, ksrc, 0.0).astype(BF16)
            vz = jnp.where(half == p, vsrc, 0.0).astype(BF16)
            for u in range(grp // heads_per_tile):
                h = g * grp + heads_per_tile * u + p
                t = h // heads_per_tile
                sc = _dot_nt(qn[t], kz) + bias_ref[h]
                sink = sink_ref[h]
                m = jnp.maximum(jnp.max(sc, axis=1, keepdims=True), sink)
                pr = jnp.exp2(sc - m)
                den = jnp.sum(pr, axis=1, keepdims=True) + jnp.exp2(sink - m)
                o = _dot(pr.astype(BF16), vz) / den
                acc[t] = o if acc[t] is None else acc[t] + o
    for t in range(n_tiles):
        y_ref[:, t * LANES:(t + 1) * LANES] = acc[t].astype(BF16)


def _swa(pm, bias, gq, gk, sinks, ind):
    bsz, seq, _ = pm.shape
    wq = A_HEADS * A_HD
    MIX_NB = SWA_NB
    rows, full, grid = _mixer_specs(bsz, seq, MIX_NB)
    cur = lambda cb: rows(LANES, cb)
    prv = lambda cb: pl.BlockSpec((MIX_NB, BLK, LANES), lambda b, c: (b, jnp.maximum(c - 1, 0), cb))
    return pl.pallas_call(
        _per_batch(_swa_body, 5, 0),
        grid=grid,
        in_specs=[
            rows(wq, C_AQ // wq),
            cur(C_AK // LANES), prv(C_AK // LANES), cur(C_AV // LANES), prv(C_AV // LANES),
            pl.BlockSpec((None, A_HEADS, BLK, 2 * BLK), lambda b, c: (jnp.minimum(c, 1), 0, 0, 0)),
            full((1, wq)), full((1, LANES)),
            pl.BlockSpec(memory_space=pltpu.SMEM),
            full((LANES, LANES)),
        ],
        out_specs=rows(wq, 0),
        out_shape=jax.ShapeDtypeStruct((bsz, seq, wq), BF16),
        compiler_params=_cparams(("arbitrary", "arbitrary")),
        name="swa",
    )(pm, pm, pm, pm, pm, bias, gq, gk, sinks, ind)


def _outproj_kernel(ym_ref, yg_ref, ya_ref, w_ref, x_ref, gt_ref, o_ref):
    y = jnp.concatenate([ym_ref[...], yg_ref[...], ya_ref[...]], axis=1)
    o_ref[...] = x_ref[...] + gt_ref[...] * _dot(y, w_ref[...])


def _outproj(ym, yg, ya, w, x2, mod4, seq):
    n, d = x2.shape
    tm, tn = 1024, d
    per_b = seq // tm
    return pl.pallas_call(
        _outproj_kernel,
        grid=(n // tm, d // tn),
        in_specs=[
            pl.BlockSpec((tm, ym.shape[1]), lambda i, j: (i, 0)),
            pl.BlockSpec((tm, yg.shape[1]), lambda i, j: (i, 0)),
            pl.BlockSpec((tm, ya.shape[1]), lambda i, j: (i, 0)),
            pl.BlockSpec((d, tn), lambda i, j: (0, j), pipeline_mode=pl.Buffered(1)),
            pl.BlockSpec((tm, tn), lambda i, j: (i, j)),
            pl.BlockSpec((None, None, 1, tn), lambda i, j: (i // per_b, 2, 0, j)),
        ],
        out_specs=pl.BlockSpec((tm, tn), lambda i, j: (i, j)),
        out_shape=jax.ShapeDtypeStruct((n, d), F32),
        compiler_params=_cparams(("arbitrary", "arbitrary")),
        name="out_proj",
    )(ym, yg, ya, w, x2, mod4)


def _swiglu_tile(h, wg_ref, wu_ref, wd_ref):
    gate = _dot(h, wg_ref[...].astype(BF16))
    up = _dot(h, wu_ref[...].astype(BF16))
    a = (gate * _sigmoid(gate) * up).astype(BF16)
    return _dot(a, wd_ref[...].astype(BF16))


def _ffn_kernel(x_ref, g_ref, sc_ref, sh_ref, gt_ref, wg_ref, wu_ref, wd_ref, o_ref, h_ref):
    f = pl.program_id(1)

    @pl.when(f == 0)
    def _():
        _norm_mod_rows(x_ref, g_ref, sc_ref, sh_ref, [h_ref])
        o_ref[...] = jnp.zeros_like(o_ref)

    o_ref[...] += _swiglu_tile(h_ref[...], wg_ref, wu_ref, wd_ref)

    @pl.when(f == pl.num_programs(1) - 1)
    def _():
        o_ref[...] = x_ref[...] + gt_ref[...] * o_ref[...]


def _ffn(x2, g, mod4, wg, wu, wd, seq):
    n, d = x2.shape
    dff = wg.shape[1]
    tm, tf = 1024, 512
    per_b = seq // tm
    mod_spec = lambda k: pl.BlockSpec((None, None, 1, d), lambda i, f: (i // per_b, k, 0, 0))
    return pl.pallas_call(
        _ffn_kernel,
        grid=(n // tm, dff // tf),
        in_specs=[
            pl.BlockSpec((tm, d), lambda i, f: (i, 0), pipeline_mode=pl.Buffered(1)),
            pl.BlockSpec((1, d), lambda i, f: (0, 0)),
            mod_spec(4), mod_spec(3), mod_spec(5),
            pl.BlockSpec((d, tf), lambda i, f: (0, f)),
            pl.BlockSpec((d, tf), lambda i, f: (0, f)),
            pl.BlockSpec((tf, d), lambda i, f: (f, 0)),
        ],
        out_specs=pl.BlockSpec((tm, d), lambda i, f: (i, 0), pipeline_mode=pl.Buffered(1)),
        out_shape=jax.ShapeDtypeStruct((n, d), F32),
        scratch_shapes=[pltpu.VMEM((tm, d), BF16)],
        compiler_params=_cparams(("arbitrary", "arbitrary")),
        name="ffn_dense",
    )(x2, g, mod4, mod4, mod4, wg, wu, wd)


def _pack_bf16_pairs(h):
    w = h.shape[1] // 2
    bits = lax.bitcast_convert_type(h.astype(BF16).astype(F32), jnp.uint32)
    return (bits[:, :w] >> 16) | (bits[:, w:] & jnp.uint32(0xFFFF0000))


def _unpack_bf16_pairs(u):
    lo = lax.bitcast_convert_type(u << 16, F32)
    hi = lax.bitcast_convert_type(u & jnp.uint32(0xFFFF0000), F32)
    return jnp.concatenate([lo, hi], axis=1).astype(BF16)


def _router_kernel(x_ref, g_ref, sc_ref, sh_ref, wh_ref, wl_ref, tri_ref, hp_ref, meta_ref, cnt_ref,
                   h_ref, run_ref):
    @pl.when(pl.program_id(0) == 0)
    def _():
        run_ref[...] = jnp.zeros_like(run_ref)

    g, sc, sh = g_ref[...], sc_ref[...], sh_ref[...]

    def norm_rows(r, carry):
        rows = pl.ds(pl.multiple_of(r * NORM_ROWS, NORM_ROWS), NORM_ROWS)
        hr = _norm_mod(x_ref[rows, :], g, sc, sh)
        h_ref[rows, :] = hr
        hp_ref[rows, :] = _pack_bf16_pairs(hr)
        return carry

    lax.fori_loop(0, x_ref.shape[0] // NORM_ROWS, norm_rows, 0, unroll=8)
    h = h_ref[...]
    h_hi = h.astype(BF16)
    h_lo = (h - h_hi.astype(F32)).astype(BF16)
    logits = _dot(h_hi, wh_ref[...]) + _dot(h_lo, wh_ref[...]) + _dot(h_hi, wl_ref[...])
    lane = lax.broadcasted_iota(I32, logits.shape, 1).astype(F32)
    l1 = jnp.where(lane < N_EXPERTS, logits, -jnp.inf)
    m1 = jnp.max(l1, axis=1, keepdims=True)
    i1 = jnp.min(jnp.where(l1 == m1, lane, float(LANES)), axis=1, keepdims=True)
    l2 = jnp.where(lane == i1, -jnp.inf, l1)
    m2 = jnp.max(l2, axis=1, keepdims=True)
    i2 = jnp.min(jnp.where(l2 == m2, lane, float(LANES)), axis=1, keepdims=True)
    e2 = jnp.exp(m2 - m1)
    w1 = 1.0 / (1.0 + e2)
    w2 = e2 / (1.0 + e2)
    member = jnp.where((lane == i1) | (lane == i2), 1.0, 0.0)
    run = run_ref[0:1, :]
    rank = _dot(tri_ref[...], member.astype(BF16)) + run
    p1 = jnp.sum(jnp.where(lane == i1, rank, 0.0), axis=1, keepdims=True)
    p2 = jnp.sum(jnp.where(lane == i2, rank, 0.0), axis=1, keepdims=True)
    run = run + jnp.sum(member, axis=0, keepdims=True)
    run_ref[0:1, :] = run
    cnt_ref[...] = jnp.broadcast_to(run, cnt_ref.shape)
    meta = jnp.zeros(logits.shape, F32)
    for idx, val in enumerate([i1, i2, w1, w2, p1, p2]):
        meta = jnp.where(lane == float(idx), val, meta)
    meta_ref[...] = meta


def _router(x2, g, mod4, wr_hi, wr_lo, seq):
    n, d = x2.shape
    tm = 512
    per_b = seq // tm
    tri = jnp.asarray(np.tril(np.ones((tm, tm), np.float32), -1), BF16)
    mod_spec = lambda k: pl.BlockSpec((None, None, 1, d), lambda i: (i // per_b, k, 0, 0))
    return pl.pallas_call(
        _router_kernel,
        grid=(n // tm,),
        in_specs=[
            pl.BlockSpec((tm, d), lambda i: (i, 0)),
            pl.BlockSpec((1, d), lambda i: (0, 0)),
            mod_spec(4), mod_spec(3),
            pl.BlockSpec((d, LANES), lambda i: (0, 0)),
            pl.BlockSpec((d, LANES), lambda i: (0, 0)),
            pl.BlockSpec((tm, tm), lambda i: (0, 0)),
        ],
        out_specs=[
            pl.BlockSpec((tm, d // 2), lambda i: (i, 0)),
            pl.BlockSpec((tm, LANES), lambda i: (i, 0)),
            pl.BlockSpec((8, LANES), lambda i: (0, 0)),
        ],
        out_shape=[jax.ShapeDtypeStruct((n, d // 2), jnp.uint32),
                   jax.ShapeDtypeStruct((n, LANES), F32),
                   jax.ShapeDtypeStruct((8, LANES), F32)],
        scratch_shapes=[pltpu.VMEM((tm, d), F32), pltpu.VMEM((8, LANES), F32)],
        compiler_params=_cparams(("arbitrary",)),
        name="router",
    )(x2, g, mod4, mod4, wr_hi, wr_lo, tri)


MOE_TM = 1024
DISP_TM = 512


def _dispatch_kernel(dest_hbm, hp_ref, xs_in_hbm, xs_hbm, idx_ref, sem_i, sem_r):
    del xs_in_hbm
    i = pl.program_id(0)
    n_idx = 2 * DISP_TM
    cp = pltpu.make_async_copy(dest_hbm.at[pl.ds(pl.multiple_of(i * n_idx, n_idx), n_idx)], idx_ref, sem_i)
    cp.start()
    cp.wait()

    def row_copy(r, k):
        return pltpu.make_async_copy(hp_ref.at[pl.ds(r, 1), :],
                                     xs_hbm.at[pl.ds(idx_ref[k * DISP_TM + r], 1), :], sem_r)

    def start(r, c):
        row_copy(r, 0).start()
        row_copy(r, 1).start()
        return c

    def wait(r, c):
        row_copy(r, 0).wait()
        row_copy(r, 1).wait()
        return c

    lax.fori_loop(0, DISP_TM, start, 0, unroll=8)
    lax.fori_loop(0, DISP_TM, wait, 0, unroll=8)


def _dispatch(dest_tiles, hp, n_rows):
    n, w = hp.shape
    xs0 = jnp.zeros((n_rows, w), jnp.uint32)
    return pl.pallas_call(
        _dispatch_kernel,
        grid=(n // DISP_TM,),
        in_specs=[
            pl.BlockSpec(memory_space=pl.ANY),
            pl.BlockSpec((DISP_TM, w), lambda i: (i, 0)),
            pl.BlockSpec(memory_space=pl.ANY),
        ],
        out_specs=pl.BlockSpec(memory_space=pl.ANY),
        out_shape=jax.ShapeDtypeStruct((n_rows, w), jnp.uint32),
        scratch_shapes=[
            pltpu.SMEM((2 * DISP_TM,), I32),
            pltpu.SemaphoreType.DMA(()),
            pltpu.SemaphoreType.DMA(()),
        ],
        input_output_aliases={2: 0},
        compiler_params=_cparams(("arbitrary",)),
        name="moe_dispatch",
    )(dest_tiles, hp, xs0)


MOE_SUB = 256


def _moe_kernel(te_ref, nu_ref, tr_ref, xs_ref, wg_ref, wu_ref, wd_ref, o_ref, xb_ref):
    t = pl.program_id(0)
    f = pl.program_id(1)
    rows = tr_ref[t]

    @pl.when(f == 0)
    def _():
        o_ref[...] = jnp.zeros_like(o_ref)

    @pl.when((rows > 0) & (f == 0))
    def _():
        xb_ref[...] = _unpack_bf16_pairs(xs_ref[...])

    @pl.when(rows == MOE_TM)
    def _():
        o_ref[...] += _swiglu_tile(xb_ref[...], wg_ref, wu_ref, wd_ref)

    for sb in range(MOE_TM // MOE_SUB):
        @pl.when((rows < MOE_TM) & (rows > sb * MOE_SUB))
        def _(sb=sb):
            sl = slice(sb * MOE_SUB, (sb + 1) * MOE_SUB)
            o_ref[sl, :] += _swiglu_tile(xb_ref[sl, :], wg_ref, wu_ref, wd_ref)


def _moe(tile_expert, n_used, tile_rows, xs, wg, wu, wd):
    d = wg.shape[1]
    n_tiles = tile_expert.shape[0]
    dff = wg.shape[2]
    tf = 512
    nf = dff // tf

    def f_blk(t, f, nu):
        return jnp.where(t < nu[0], f, nf - 1)

    grid_spec = pltpu.PrefetchScalarGridSpec(
        num_scalar_prefetch=3,
        grid=(n_tiles, nf),
        in_specs=[
            pl.BlockSpec((MOE_TM, d // 2), lambda t, f, te, nu, tr: (jnp.minimum(t, nu[0] - 1), 0)),
            pl.BlockSpec((None, d, tf), lambda t, f, te, nu, tr: (te[t], 0, f_blk(t, f, nu))),
            pl.BlockSpec((None, d, tf), lambda t, f, te, nu, tr: (te[t], 0, f_blk(t, f, nu))),
            pl.BlockSpec((None, tf, d), lambda t, f, te, nu, tr: (te[t], f_blk(t, f, nu), 0)),
        ],
        out_specs=pl.BlockSpec((MOE_TM, d), lambda t, f, te, nu, tr: (t, 0), pipeline_mode=pl.Buffered(1)),
        scratch_shapes=[pltpu.VMEM((MOE_TM, d), BF16)],
    )
    return pl.pallas_call(
        _moe_kernel,
        grid_spec=grid_spec,
        out_shape=jax.ShapeDtypeStruct((n_tiles * MOE_TM, d), F32),
        compiler_params=_cparams(("arbitrary", "arbitrary")),
        name="moe_experts",
    )(tile_expert, n_used, tile_rows, xs, wg, wu, wd)


COMB_TM = 512


def _combine_kernel(dest_hbm, y_hbm, x_ref, meta_ref, gt_ref, o_ref, idx_ref, y_ref, sem_i, sem_r):
    i = pl.program_id(0)
    n_idx = 2 * COMB_TM

    def row_copy(slot, j):
        return pltpu.make_async_copy(
            y_hbm.at[pl.ds(idx_ref[slot * n_idx + j], 1), :],
            y_ref.at[slot, pl.ds(j, 1), :], sem_r.at[slot])

    def fetch(step, slot):
        cp = pltpu.make_async_copy(
            dest_hbm.at[pl.ds(pl.multiple_of(step * n_idx, n_idx), n_idx)],
            idx_ref.at[pl.ds(pl.multiple_of(slot * n_idx, n_idx), n_idx)], sem_i)
        cp.start()
        cp.wait()

        def start(j, c):
            row_copy(slot, j).start()
            return c

        lax.fori_loop(0, n_idx, start, 0, unroll=8)

    @pl.when(i == 0)
    def _():
        fetch(i, 0)

    @pl.when(i + 1 < pl.num_programs(0))
    def _():
        fetch(i + 1, (i + 1) % 2)

    slot = i % 2

    def wait(j, c):
        row_copy(slot, j).wait()
        return c

    lax.fori_loop(0, n_idx, wait, 0, unroll=8)
    w1 = meta_ref[:, 2:3]
    w2 = meta_ref[:, 3:4]
    y1 = y_ref[slot, 0:COMB_TM, :]
    y2 = y_ref[slot, COMB_TM:n_idx, :]
    o_ref[...] = x_ref[...] + gt_ref[...] * (w1 * y1 + w2 * y2)


def _combine(dest, y_sorted, x2, meta, mod4, seq):
    n, d = x2.shape
    tm = COMB_TM
    per_b = seq // tm
    return pl.pallas_call(
        _combine_kernel,
        grid=(n // tm,),
        in_specs=[
            pl.BlockSpec(memory_space=pl.ANY),
            pl.BlockSpec(memory_space=pl.ANY),
            pl.BlockSpec((tm, d), lambda i: (i, 0)),
            pl.BlockSpec((tm, LANES), lambda i: (i, 0)),
            pl.BlockSpec((None, None, 1, d), lambda i: (i // per_b, 5, 0, 0)),
        ],
        out_specs=pl.BlockSpec((tm, d), lambda i: (i, 0)),
        out_shape=jax.ShapeDtypeStruct((n, d), F32),
        scratch_shapes=[
            pltpu.SMEM((2 * 2 * tm,), I32),
            pltpu.VMEM((2, 2 * tm, d), F32),
            pltpu.SemaphoreType.DMA(()),
            pltpu.SemaphoreType.DMA((2,)),
        ],
        compiler_params=_cparams(("arbitrary",)),
        name="moe_combine",
    )(dest, y_sorted, x2, meta, mod4)


def _moe_layer(x2, g, mod4, w_router, wg, wu, wd, seq):
    n, d = x2.shape
    wr = jnp.zeros((d, LANES), F32).at[:, :N_EXPERTS].set(w_router)
    wr_hi = wr.astype(BF16)
    wr_lo = (wr - wr_hi.astype(F32)).astype(BF16)
    hp, meta, cnt = _router(x2, g, mod4, wr_hi, wr_lo, seq)

    n_tiles = 2 * n // MOE_TM + N_EXPERTS
    counts = cnt[0, :N_EXPERTS].astype(I32)
    tiles_per = (counts + MOE_TM - 1) // MOE_TM
    tile_end = jnp.cumsum(tiles_per)
    row_start = (tile_end - tiles_per) * MOE_TM
    n_used = tile_end[-1]
    idx = meta[:, 0:2].astype(I32)
    pos = meta[:, 4:6].astype(I32)
    start_of = sum(jnp.where(idx == e, row_start[e], 0) for e in range(N_EXPERTS))
    dest = start_of + pos
    all_tiles = jnp.arange(n_tiles, dtype=I32)
    tile_ids = jnp.minimum(all_tiles, n_used - 1)
    tile_expert = jnp.sum(tile_ids[:, None] >= tile_end[None, :], axis=1).astype(I32)
    onehot_e = tile_expert[:, None] == jnp.arange(N_EXPERTS, dtype=I32)[None, :]
    first_tile = jnp.sum(jnp.where(onehot_e, (tile_end - tiles_per)[None, :], 0), axis=1)
    cnt_tile = jnp.sum(jnp.where(onehot_e, counts[None, :], 0), axis=1)
    tile_rows = jnp.clip(cnt_tile - (all_tiles - first_tile) * MOE_TM, 0, MOE_TM)
    tile_rows = jnp.where(all_tiles < n_used, tile_rows, 0).astype(I32)
    assert COMB_TM == DISP_TM
    dest_tiles = dest.reshape(n // COMB_TM, COMB_TM, 2).transpose(0, 2, 1).reshape(-1)

    xs = _dispatch(dest_tiles, hp, n_tiles * MOE_TM)
    y_sorted = _moe(tile_expert, n_used.reshape(1).astype(I32), tile_rows, xs, wg, wu, wd)
    return _combine(dest_tiles, y_sorted, x2, meta, mod4, seq)


def _t5_bucket(dist):
    max_exact = N_BUCKETS // 2
    d = np.maximum(dist, 1).astype(np.float32)
    large = max_exact + (np.log(d / max_exact) / np.log(MAX_DIST / max_exact)
                         * (N_BUCKETS - max_exact)).astype(np.int32)
    large = np.minimum(large, N_BUCKETS - 1)
    return np.where(dist < max_exact, dist, large).astype(np.int32)


def _pack_w_in(w):
    splits = [M_HEADS * M_QK, M_HEADS * M_QK, M_HEADS * M_V, M_HEADS * M_V, 2 * M_HEADS,
              G_HEADS * G_QK, G_HEADS * G_QK, G_HEADS * G_V, G_HEADS * G_V, G_RANK,
              A_HEADS * A_HD, A_KV * A_HD, A_KV * A_HD]
    offs = np.concatenate([[0], np.cumsum(splits)])
    seg = lambda i: w[:, offs[i]:offs[i + 1]]
    main = jnp.concatenate([seg(i) for i in (0, 1, 2, 3, 5, 6, 7, 8, 10, 11, 12)], axis=1)
    small = jnp.concatenate(
        [seg(4), seg(9), jnp.zeros((w.shape[0], LANES - 2 * M_HEADS - G_RANK), w.dtype)], axis=1)
    return main.astype(BF16), small.astype(BF16)


def kernel(x, c, w_ada, b_ada, g_mix_norm, g_ffn_norm, w_in, b_gates_m, w_conv_m, b_conv_m, g_out_m,
           w_gla_decay, b_gla_decay, g_out_g, g_qnorm, g_knorm, sinks, rel_bias, w_out, w_ffn_gate,
           w_ffn_up, w_ffn_down, w_router, w_moe_gate, w_moe_up, w_moe_down):
    bsz, seq, d = x.shape
    depth = w_ada.shape[0]
    n = bsz * seq
    assert seq % 1024 == 0 and d == 2048 and n % MOE_TM == 0 and bsz % max(MLSTM_NB, GLA_NB, SWA_NB) == 0

    mod = _ada(c, w_ada, b_ada)

    jj = np.arange(BLK)[:, None]
    ss = np.arange(2 * BLK)[None, :]
    buckets = _t5_bucket(np.clip(jj + BLK - ss, 0, None))
    onehot = jnp.asarray(np.eye(N_BUCKETS, dtype=np.float32)[buckets.reshape(-1)])
    bias = jnp.einsum("pb,bh->hp", onehot, rel_bias.astype(F32),
                      precision=lax.Precision.HIGHEST).reshape(A_HEADS, BLK, 2 * BLK)
    in_window = (jj + BLK - ss >= 0) & (jj + BLK - ss < WINDOW)
    masks = np.stack([in_window & (ss >= BLK), in_window])
    bias = jnp.where(jnp.asarray(masks)[:, None], (bias * LOG2E)[None], -jnp.inf)
    ind = jnp.asarray(np.kron(np.eye(LANES // A_HD), np.ones((A_HD, A_HD))) / A_HD, BF16)

    x2 = x.reshape(n, d)
    for l in range(depth):
        mod4 = mod[l].reshape(bsz, 6, 1, d)
        w_main, w_small = _pack_w_in(w_in[l])
        pm, ps = _inproj(x2, g_mix_norm[l].reshape(1, d), mod4, w_main, w_small, seq)
        pm = pm.reshape(bsz, seq, -1)
        ps = ps.reshape(bsz, seq, -1)

        gb = jnp.zeros((1, LANES), F32).at[0, :2 * M_HEADS].set(b_gates_m[l].reshape(-1))
        ym = _mlstm(pm, ps, w_conv_m[l], b_conv_m[l].reshape(1, -1), gb, g_out_m[l].reshape(1, -1))
        wdec = jnp.zeros((LANES, G_HEADS * G_QK), F32).at[S_GA:S_GA + G_RANK].set(w_gla_decay[l])
        yg = _gla(pm, ps, wdec, b_gla_decay[l].reshape(1, -1), g_out_g[l].reshape(1, -1))
        gq = jnp.tile(g_qnorm[l], A_HEADS).reshape(1, -1) * (A_HD ** -0.5 * LOG2E)
        gk = jnp.tile(g_knorm[l], A_KV).reshape(1, -1)
        ya = _swa(pm, bias, gq, gk, sinks[l] * LOG2E, ind)
        x2 = _outproj(ym.reshape(n, -1), yg.reshape(n, -1), ya.reshape(n, -1), w_out[l].astype(BF16),
                      x2, mod4, seq)

        gf = g_ffn_norm[l].reshape(1, d)
        if l % 2 == 0:
            j = l // 2
            x2 = _ffn(x2, gf, mod4, w_ffn_gate[j], w_ffn_up[j], w_ffn_down[j], seq)
        else:
            j = l // 2
            x2 = _moe_layer(x2, gf, mod4, w_router[j], w_moe_gate[j], w_moe_up[j], w_moe_down[j], seq)
    return x2.reshape(bsz, seq, d)
```

```python
import functools

import numpy as np
import jax
import jax.numpy as jnp
from jax import lax
from jax.experimental import pallas as pl
from jax.experimental.pallas import tpu as pltpu

F32 = jnp.float32
BF16 = jnp.bfloat16
I32 = jnp.int32

M_HEADS = 4
M_QK = 64
M_V = 128
CONV_W = 4
G_HEADS = 4
G_QK = 64
G_V = 128
G_RANK = 16
G_TAU = 16.0
G_CHUNK = 64
G_SUB = 16
A_HD = 64
A_HEADS = 16
A_KV = 2
WINDOW = 128
N_BUCKETS = 32
MAX_DIST = 128
N_EXPERTS = 8
EPS = 1e-6
LOG2E = 1.4426950408889634

LANES = 128
BLK = 128
VMEM_LIMIT = 60000 * 1024

C_MQK, C_MV, C_MO = 0, 512, 1024
C_GQK, C_GV, C_GG = 1536, 2048, 2560
C_AQ, C_AK, C_AV = 3072, 4096, 4224
N_MAIN = 4352
S_GA = 8


def _cparams(sem):
    return pltpu.CompilerParams(dimension_semantics=sem, vmem_limit_bytes=VMEM_LIMIT)


def _dot(a, b):
    return jnp.dot(a, b, preferred_element_type=F32)


def _dot_nt(a, b):
    return lax.dot_general(a, b, (((1,), (1,)), ((), ())), preferred_element_type=F32)


def _split_dot(a, b_f32):
    hi = b_f32.astype(BF16)
    lo = (b_f32 - hi.astype(F32)).astype(BF16)
    return _dot(a, hi) + _dot(a, lo)


def _sigmoid(x):
    return 1.0 / (1.0 + jnp.exp(-x))


def _log_sigmoid(x):
    return jnp.minimum(x, 0.0) - jnp.log(1.0 + jnp.exp(-jnp.abs(x)))


def _norm_mod(x, g, sc, sh):
    ms = jnp.mean(x * x, axis=-1, keepdims=True)
    return (x * lax.rsqrt(ms + EPS) * g) * (1.0 + sc) + sh


NORM_ROWS = 16


def _norm_mod_rows(x_ref, g_ref, sc_ref, sh_ref, out_refs):
    g, sc, sh = g_ref[...], sc_ref[...], sh_ref[...]

    def body(r, carry):
        rows = pl.ds(pl.multiple_of(r * NORM_ROWS, NORM_ROWS), NORM_ROWS)
        h = _norm_mod(x_ref[rows, :], g, sc, sh)
        for ref in out_refs:
            ref[rows, :] = h.astype(ref.dtype)
        return carry

    lax.fori_loop(0, x_ref.shape[0] // NORM_ROWS, body, 0, unroll=8)


def _ada_kernel(c_ref, w_ref, b_ref, o_ref):
    c = c_ref[...]
    cond = (c * _sigmoid(c)).astype(BF16)
    o_ref[...] = _dot(cond, w_ref[...].astype(BF16)) + b_ref[...]


def _ada(c, w_ada, b_ada):
    depth, d, n6 = w_ada.shape
    bsz = c.shape[0]
    tn = 1024
    return pl.pallas_call(
        _ada_kernel,
        grid=(depth, n6 // tn),
        in_specs=[
            pl.BlockSpec((bsz, d), lambda l, n: (0, 0)),
            pl.BlockSpec((None, d, tn), lambda l, n: (l, 0, n)),
            pl.BlockSpec((None, 1, tn), lambda l, n: (l, 0, n)),
        ],
        out_specs=pl.BlockSpec((None, bsz, tn), lambda l, n: (l, 0, n)),
        out_shape=jax.ShapeDtypeStruct((depth, bsz, n6), F32),
        compiler_params=_cparams(("arbitrary", "arbitrary")),
        name="ada_mod",
    )(c, w_ada, b_ada.reshape(depth, 1, n6))


def _inproj_kernel(x_ref, g_ref, sc_ref, sh_ref, wm_ref, ws_ref, om_ref, os_ref, h_ref):
    @pl.when(pl.program_id(1) == 0)
    def _():
        _norm_mod_rows(x_ref, g_ref, sc_ref, sh_ref, [h_ref])
        os_ref[...] = _dot(h_ref[...], ws_ref[...])

    om_ref[...] = _dot(h_ref[...], wm_ref[...]).astype(BF16)


def _inproj(x2, g, mod4, w_main, w_small, seq):
    n, d = x2.shape
    tm, tn = 1024, N_MAIN // 2
    per_b = seq // tm
    mod_spec = lambda k: pl.BlockSpec((None, None, 1, d), lambda i, j: (i // per_b, k, 0, 0))
    return pl.pallas_call(
        _inproj_kernel,
        grid=(n // tm, N_MAIN // tn),
        in_specs=[
            pl.BlockSpec((tm, d), lambda i, j: (i, 0)),
            pl.BlockSpec((1, d), lambda i, j: (0, 0)),
            mod_spec(1), mod_spec(0),
            pl.BlockSpec((d, tn), lambda i, j: (0, j)),
            pl.BlockSpec((d, LANES), lambda i, j: (0, 0)),
        ],
        out_specs=[
            pl.BlockSpec((tm, tn), lambda i, j: (i, j)),
            pl.BlockSpec((tm, LANES), lambda i, j: (i, 0)),
        ],
        out_shape=[jax.ShapeDtypeStruct((n, N_MAIN), BF16),
                   jax.ShapeDtypeStruct((n, LANES), F32)],
        scratch_shapes=[pltpu.VMEM((tm, d), BF16)],
        compiler_params=_cparams(("arbitrary", "arbitrary")),
        name="in_proj",
    )(x2, g, mod4, mod4, w_main, w_small)


def _per_batch(body, n_batch_refs, n_state_refs):
    def kern(*refs):
        n_shared = len(refs) - n_batch_refs - n_state_refs - 1
        batch_in, shared = refs[:n_batch_refs], refs[n_batch_refs:n_batch_refs + n_shared]
        y_ref = refs[n_batch_refs + n_shared]
        state = refs[len(refs) - n_state_refs:]

        @pl.when(pl.program_id(1) == 0)
        def _():
            for st in state:
                st[...] = jnp.zeros_like(st)

        for bb in range(y_ref.shape[0]):
            body(*[r.at[bb] for r in batch_in], *shared, y_ref.at[bb], *[st.at[bb] for st in state])
    return kern


def _mlstm_body(qk_ref, v_ref, o_ref, sm_ref, wc_ref, bc_ref, gb_ref, go_ref, y_ref,
                tail_ref, ct_ref, n_ref, m_ref):
    L = BLK
    nqk = M_HEADS * M_QK

    x = qk_ref[...].astype(F32)
    prev = tail_ref[...]
    row8 = lax.broadcasted_iota(I32, prev.shape, 0)
    acc = x * wc_ref[CONV_W - 1:CONV_W, :] + bc_ref[...]
    for dlt in range(1, CONV_W):
        xs = pltpu.roll(x, dlt, axis=0)
        ps = pltpu.roll(prev, dlt, axis=0)
        top = jnp.where(row8 < dlt, ps, xs[0:8])
        xs = jnp.concatenate([top, xs[8:]], axis=0)
        acc = acc + xs * wc_ref[CONV_W - 1 - dlt:CONV_W - dlt, :]
    tail_ref[...] = x[L - 8:L]
    qk = acc * _sigmoid(acc)
    q = qk[:, :nqk]
    k = qk[:, nqk:] * (M_QK ** -0.5)
    kt_b = k.T.astype(BF16)

    gates = sm_ref[...] + gb_ref[...]
    row = lax.broadcasted_iota(I32, (L, L), 0)
    col = lax.broadcasted_iota(I32, (L, L), 1)
    causal = row >= col
    tri = causal.astype(BF16)
    bcum = _split_dot(tri, _log_sigmoid(gates))
    z = jnp.where(col < M_HEADS, gates, bcum)
    zt = z.T
    lane_q = lax.broadcasted_iota(I32, (L, nqk), 1)

    n_all = n_ref[...]
    head_n = lax.broadcasted_iota(I32, n_all.shape, 1) // M_QK
    row_n = lax.broadcasted_iota(I32, n_all.shape, 0)
    qn = _dot_nt(q.astype(BF16), jnp.where(head_n == row_n, n_all, 0.0).astype(BF16))
    lane_w = lax.broadcasted_iota(I32, (L, LANES), 1)
    w_cols = jnp.zeros((L, LANES), F32)
    decay_rows = jnp.zeros((n_all.shape[0], 1), F32)

    outs = []
    for h in range(M_HEADS):
        in_head = (lane_q >= h * M_QK) & (lane_q < (h + 1) * M_QK)
        qm = jnp.where(in_head, q, 0.0)
        qm_b = qm.astype(BF16)
        v_h = v_ref[:, h * M_V:(h + 1) * M_V]
        m_prev = m_ref[h:h + 1, 0:1]
        ct_h = ct_ref[h]
        bcol = z[:, M_HEADS + h:M_HEADS + h + 1]
        icol = z[:, h:h + 1]
        brow = zt[M_HEADS + h:M_HEADS + h + 1, :]
        irow = zt[h:h + 1, :]
        logd = jnp.where(causal, bcol - brow + irow, -jnp.inf)
        log_inter = bcol + m_prev
        m_row = jnp.maximum(log_inter, jnp.max(logd, axis=1, keepdims=True))
        dmat = jnp.exp(logd - m_row)
        w_inter = jnp.exp(log_inter - m_row)
        s = _dot(qm_b, kt_b) * dmat
        num = w_inter * _dot(qm_b, ct_h.astype(BF16)) + _dot(s.astype(BF16), v_h)
        den = w_inter * qn[:, h:h + 1] + jnp.sum(s, axis=1, keepdims=True)
        outs.append(num / jnp.maximum(jnp.abs(den), jnp.exp(-m_row)))
        b_last = bcol[L - 1:L, :]
        log_w = b_last - bcol + icol
        m_new = jnp.maximum(b_last + m_prev, jnp.max(log_w, axis=0, keepdims=True))
        w = jnp.exp(log_w - m_new)
        decay = jnp.exp(b_last + m_prev - m_new)
        vw = (v_h.astype(F32) * w).astype(BF16)
        ct_ref[h] = decay * ct_h + _dot(kt_b, vw)
        w_cols = jnp.where(lane_w == h, w, w_cols)
        decay_rows = jnp.where(row_n[:, 0:1] == h, decay, decay_rows)
        m_ref[h:h + 1, :] = jnp.broadcast_to(m_new, (1, LANES))

    n_upd = _dot(w_cols.T[0:n_all.shape[0]].astype(BF16), k.astype(BF16))
    n_ref[...] = decay_rows * n_all + n_upd

    og = _sigmoid(o_ref[...].astype(F32))
    for h in range(M_HEADS):
        hh = outs[h]
        ms = jnp.mean(hh * hh, axis=1, keepdims=True)
        sl = slice(h * M_V, (h + 1) * M_V)
        y = hh * lax.rsqrt(ms + EPS) * go_ref[:, sl]
        y_ref[:, sl] = (y * og[:, sl]).astype(BF16)


MLSTM_NB, GLA_NB, SWA_NB = 1, 2, 2


def _mixer_specs(bsz, seq, nb):
    rows = lambda width, cb: pl.BlockSpec((nb, BLK, width), lambda b, c: (b, c, cb))
    full = lambda shp: pl.BlockSpec(shp, lambda b, c: (0,) * len(shp))
    return rows, full, (bsz // nb, seq // BLK)


def _mlstm(pm, ps, wc, bc, gb, go):
    bsz, seq, _ = pm.shape
    wd = M_HEADS * M_V
    MIX_NB = MLSTM_NB
    rows, full, grid = _mixer_specs(bsz, seq, MIX_NB)
    return pl.pallas_call(
        _per_batch(_mlstm_body, 4, 4),
        grid=grid,
        in_specs=[
            rows(wd, C_MQK // wd), rows(wd, C_MV // wd), rows(wd, C_MO // wd), rows(LANES, 0),
            full((CONV_W, 2 * M_HEADS * M_QK)), full((1, 2 * M_HEADS * M_QK)),
            full((1, LANES)), full((1, wd)),
        ],
        out_specs=rows(wd, 0),
        out_shape=jax.ShapeDtypeStruct((bsz, seq, wd), BF16),
        scratch_shapes=[
            pltpu.VMEM((MIX_NB, 8, 2 * M_HEADS * M_QK), F32),
            pltpu.VMEM((MIX_NB, M_HEADS, M_HEADS * M_QK, M_V), F32),
            pltpu.VMEM((MIX_NB, 8, M_HEADS * M_QK), F32),
            pltpu.VMEM((MIX_NB, 8, LANES), F32),
        ],
        compiler_params=_cparams(("arbitrary", "arbitrary")),
        name="mlstm",
    )(pm, pm, pm, ps, wc, bc, gb, go)


def _gla_body(qk_ref, v_ref, g_ref, sm_ref, wd_ref, bd_ref, go_ref, y_ref, s_ref):
    L, C, SC = BLK, G_CHUNK, G_SUB
    nqk = G_HEADS * G_QK
    pair_k = 2 * G_QK
    pair_v = 2 * G_V
    n_pairs = G_HEADS // 2

    zdec = _dot(sm_ref[...].astype(BF16), wd_ref[...].astype(BF16)) + bd_ref[...]
    log_a = _log_sigmoid(zdec) / G_TAU
    row = lax.broadcasted_iota(I32, (L, L), 0)
    col = lax.broadcasted_iota(I32, (L, L), 1)
    same_chunk = (row // C) == (col // C)
    tri = ((row >= col) & same_chunk).astype(BF16)
    b = _split_dot(tri, log_a)
    bt = b.T

    q = qk_ref[:, :nqk].astype(F32) * (G_QK ** -0.5)
    k = qk_ref[:, nqk:].astype(F32)
    b_end = jnp.concatenate(
        [jnp.broadcast_to(b[(cc + 1) * C - 1:(cc + 1) * C, :], (C, nqk)) for cc in range(L // C)], axis=0)
    kdt = (k * jnp.exp(b_end - b)).T
    qe = q * jnp.exp(b)

    dloc = col - (row // SC) * SC
    diag_sel = jnp.where((dloc >= 0) & (dloc <= row % SC), dloc, -1)
    sub_r = (row % C) // SC
    off_sel = jnp.where(same_chunk & ((col % C) // SC < sub_r), sub_r, -1)
    diag_sel2 = jnp.concatenate([diag_sel, diag_sel], axis=1)
    off_sel2 = jnp.concatenate([off_sel, off_sel], axis=1)

    lane_s = lax.broadcasted_iota(I32, (pair_k, L), 1)
    head_k = lax.broadcasted_iota(I32, (L, pair_k), 1) // G_QK
    rblk = lax.broadcasted_iota(I32, (pair_k, pair_v), 0) // G_QK
    cblk = lax.broadcasted_iota(I32, (pair_k, pair_v), 1) // G_V
    blockdiag = rblk == cblk
    vrow = lax.broadcasted_iota(I32, (2 * L, pair_v), 0) // L
    vcol = lax.broadcasted_iota(I32, (2 * L, pair_v), 1) // G_V
    vdiag = vrow == vcol

    def stack_heads(x):
        return jnp.concatenate([jnp.where(head_k == hh, x, 0.0) for hh in range(2)], axis=0).astype(BF16)

    pair_out = []
    for p in range(n_pairs):
        kl = slice(p * pair_k, (p + 1) * pair_k)
        vl = slice(p * pair_v, (p + 1) * pair_v)
        bp, qp, kp = b[:, kl], q[:, kl], k[:, kl]
        v_p = v_ref[:, vl]
        b3 = bp.reshape(L // SC, SC, pair_k)

        def sub_row(i, b3=b3):
            return jnp.broadcast_to(b3[:, i:i + 1, :], b3.shape).reshape(L, pair_k)

        kst = stack_heads(kp)
        att = jnp.zeros((L, 2 * L), F32)
        for i in range(SC):
            e = jnp.exp(jnp.minimum(bp - sub_row(i), 0.0))
            r = _dot_nt((qp * e).astype(BF16), kst)
            att = att + jnp.where(diag_sel2 == i, r, 0.0)

        qs = (qp * jnp.exp(bp - sub_row(0))).astype(BF16)
        for sub in range(1, C // SC):
            bref = jnp.concatenate(
                [jnp.broadcast_to(bp[cc * C + sub * SC:cc * C + sub * SC + 1, :], (C, pair_k))
                 for cc in range(L // C)], axis=0)
            ks = kp * jnp.exp(jnp.minimum(bref - bp, 0.0))
            att = att + jnp.where(off_sel2 == sub, _dot_nt(qs, stack_heads(ks)), 0.0)

        vblk = jnp.where(vdiag, jnp.concatenate([v_p, v_p], axis=0), 0)
        intra = _dot(att.astype(BF16), vblk)

        outs = []
        for cc in range(L // C):
            r0 = cc * C
            s_p = s_ref[p]
            outs.append(_dot(qe[r0:r0 + C, kl].astype(BF16), s_p.astype(BF16)) + intra[r0:r0 + C])
            in_chunk = (lane_s >= r0) & (lane_s < r0 + C)
            kd = jnp.where(in_chunk, kdt[kl, :], 0.0).astype(BF16)
            upd = jnp.where(blockdiag, _dot(kd, v_p), 0.0)
            dcol = jnp.exp(bt[kl, r0 + C - 1:r0 + C])
            s_ref[p] = dcol * s_p + upd
        pair_out.append(outs)

    gate = g_ref[...].astype(F32)
    gate = gate * _sigmoid(gate)
    for p in range(n_pairs):
        o_p = jnp.concatenate(pair_out[p], axis=0)
        for hh in range(2):
            h = 2 * p + hh
            oh = o_p[:, hh * G_V:(hh + 1) * G_V]
            ms = jnp.mean(oh * oh, axis=1, keepdims=True)
            sl = slice(h * G_V, (h + 1) * G_V)
            y = oh * lax.rsqrt(ms + EPS) * go_ref[:, sl]
            y_ref[:, sl] = (y * gate[:, sl]).astype(BF16)


def _gla(pm, ps, wd, bd, go):
    bsz, seq, _ = pm.shape
    w = G_HEADS * G_V
    nqk = G_HEADS * G_QK
    MIX_NB = GLA_NB
    rows, full, grid = _mixer_specs(bsz, seq, MIX_NB)
    return pl.pallas_call(
        _per_batch(_gla_body, 4, 1),
        grid=grid,
        in_specs=[
            rows(w, C_GQK // w), rows(w, C_GV // w), rows(w, C_GG // w), rows(LANES, 0),
            full((LANES, nqk)), full((1, nqk)), full((1, w)),
        ],
        out_specs=rows(w, 0),
        out_shape=jax.ShapeDtypeStruct((bsz, seq, w), BF16),
        scratch_shapes=[
            pltpu.VMEM((MIX_NB, G_HEADS // 2, 2 * G_QK, 2 * G_V), F32),
        ],
        compiler_params=_cparams(("arbitrary", "arbitrary")),
        name="gla",
    )(pm, pm, pm, ps, wd, bd, go)


def _group_mean_sq(x, ind):
    return _split_dot_rhs(x * x, ind)


def _split_dot_rhs(a_f32, b_bf16):
    hi = a_f32.astype(BF16)
    lo = (a_f32 - hi.astype(F32)).astype(BF16)
    return _dot(hi, b_bf16) + _dot(lo, b_bf16)


def _swa_body(q_ref, kc_ref, kp_ref, vc_ref, vp_ref, bias_ref, gq_ref, gk_ref, sink_ref, ind_ref,
              y_ref):
    T = BLK
    heads_per_tile = LANES // A_HD
    n_tiles = A_HEADS // heads_per_tile
    grp = A_HEADS // A_KV
    ind = ind_ref[...]

    qn = []
    for t in range(n_tiles):
        sl = slice(t * LANES, (t + 1) * LANES)
        qt = q_ref[:, sl].astype(F32)
        ms = _group_mean_sq(qt, ind)
        qn.append((qt * lax.rsqrt(ms + EPS) * gq_ref[:, sl]).astype(BF16))
    kk = jnp.concatenate([kp_ref[...], kc_ref[...]], axis=0).astype(F32)
    kn = kk * lax.rsqrt(_group_mean_sq(kk, ind) + EPS) * gk_ref[...]
    vv = jnp.concatenate([vp_ref[...], vc_ref[...]], axis=0).astype(F32)
    k_sw = pltpu.roll(kn, A_HD, axis=1)
    v_sw = pltpu.roll(vv, A_HD, axis=1)
    half = lax.broadcasted_iota(I32, (2 * T, LANES), 1) // A_HD

    acc = [None] * n_tiles
    for g in range(A_KV):
        for p in range(heads_per_tile):
            ksrc, vsrc = (kn, vv) if g == p else (k_sw, v_sw)
            kz = jnp.where(half == p, ksrc, 0.0).astype(BF16)
            vz = jnp.where(half == p, vsrc, 0.0).astype(BF16)
            for u in range(grp // heads_per_tile):
                h = g * grp + heads_per_tile * u + p
                t = h // heads_per_tile
                sc = _dot_nt(qn[t], kz) + bias_ref[h]
                sink = sink_ref[h]
                m = jnp.maximum(jnp.max(sc, axis=1, keepdims=True), sink)
                pr = jnp.exp2(sc - m)
                den = jnp.sum(pr, axis=1, keepdims=True) + jnp.exp2(sink - m)
                o = _dot(pr.astype(BF16), vz) / den
                acc[t] = o if acc[t] is None else acc[t] + o
    for t in range(n_tiles):
        y_ref[:, t * LANES:(t + 1) * LANES] = acc[t].astype(BF16)


def _swa(pm, bias, gq, gk, sinks, ind):
    bsz, seq, _ = pm.shape
    wq = A_HEADS * A_HD
    MIX_NB = SWA_NB
    rows, full, grid = _mixer_specs(bsz, seq, MIX_NB)
    cur = lambda cb: rows(LANES, cb)
    prv = lambda cb: pl.BlockSpec((MIX_NB, BLK, LANES), lambda b, c: (b, jnp.maximum(c - 1, 0), cb))
    return pl.pallas_call(
        _per_batch(_swa_body, 5, 0),
        grid=grid,
        in_specs=[
            rows(wq, C_AQ // wq),
            cur(C_AK // LANES), prv(C_AK // LANES), cur(C_AV // LANES), prv(C_AV // LANES),
            pl.BlockSpec((None, A_HEADS, BLK, 2 * BLK), lambda b, c: (jnp.minimum(c, 1), 0, 0, 0)),
            full((1, wq)), full((1, LANES)),
            pl.BlockSpec(memory_space=pltpu.SMEM),
            full((LANES, LANES)),
        ],
        out_specs=rows(wq, 0),
        out_shape=jax.ShapeDtypeStruct((bsz, seq, wq), BF16),
        compiler_params=_cparams(("arbitrary", "arbitrary")),
        name="swa",
    )(pm, pm, pm, pm, pm, bias, gq, gk, sinks, ind)


def _outproj_kernel(ym_ref, yg_ref, ya_ref, w_ref, x_ref, gt_ref, o_ref):
    y = jnp.concatenate([ym_ref[...], yg_ref[...], ya_ref[...]], axis=1)
    o_ref[...] = x_ref[...] + gt_ref[...] * _dot(y, w_ref[...])


def _outproj(ym, yg, ya, w, x2, mod4, seq):
    n, d = x2.shape
    tm, tn = 1024, d
    per_b = seq // tm
    return pl.pallas_call(
        _outproj_kernel,
        grid=(n // tm, d // tn),
        in_specs=[
            pl.BlockSpec((tm, ym.shape[1]), lambda i, j: (i, 0)),
            pl.BlockSpec((tm, yg.shape[1]), lambda i, j: (i, 0)),
            pl.BlockSpec((tm, ya.shape[1]), lambda i, j: (i, 0)),
            pl.BlockSpec((d, tn), lambda i, j: (0, j), pipeline_mode=pl.Buffered(1)),
            pl.BlockSpec((tm, tn), lambda i, j: (i, j)),
            pl.BlockSpec((None, None, 1, tn), lambda i, j: (i // per_b, 2, 0, j)),
        ],
        out_specs=pl.BlockSpec((tm, tn), lambda i, j: (i, j)),
        out_shape=jax.ShapeDtypeStruct((n, d), F32),
        compiler_params=_cparams(("arbitrary", "arbitrary")),
        name="out_proj",
    )(ym, yg, ya, w, x2, mod4)


def _swiglu_tile(h, wg_ref, wu_ref, wd_ref):
    gate = _dot(h, wg_ref[...].astype(BF16))
    up = _dot(h, wu_ref[...].astype(BF16))
    a = (gate * _sigmoid(gate) * up).astype(BF16)
    return _dot(a, wd_ref[...].astype(BF16))


def _ffn_kernel(x_ref, g_ref, sc_ref, sh_ref, gt_ref, wg_ref, wu_ref, wd_ref, o_ref, h_ref):
    f = pl.program_id(1)

    @pl.when(f == 0)
    def _():
        _norm_mod_rows(x_ref, g_ref, sc_ref, sh_ref, [h_ref])
        o_ref[...] = jnp.zeros_like(o_ref)

    o_ref[...] += _swiglu_tile(h_ref[...], wg_ref, wu_ref, wd_ref)

    @pl.when(f == pl.num_programs(1) - 1)
    def _():
        o_ref[...] = x_ref[...] + gt_ref[...] * o_ref[...]


def _ffn(x2, g, mod4, wg, wu, wd, seq):
    n, d = x2.shape
    dff = wg.shape[1]
    tm, tf = 1024, 512
    per_b = seq // tm
    mod_spec = lambda k: pl.BlockSpec((None, None, 1, d), lambda i, f: (i // per_b, k, 0, 0))
    return pl.pallas_call(
        _ffn_kernel,
        grid=(n // tm, dff // tf),
        in_specs=[
            pl.BlockSpec((tm, d), lambda i, f: (i, 0), pipeline_mode=pl.Buffered(1)),
            pl.BlockSpec((1, d), lambda i, f: (0, 0)),
            mod_spec(4), mod_spec(3), mod_spec(5),
            pl.BlockSpec((d, tf), lambda i, f: (0, f)),
            pl.BlockSpec((d, tf), lambda i, f: (0, f)),
            pl.BlockSpec((tf, d), lambda i, f: (f, 0)),
        ],
        out_specs=pl.BlockSpec((tm, d), lambda i, f: (i, 0)),
        out_shape=jax.ShapeDtypeStruct((n, d), F32),
        scratch_shapes=[pltpu.VMEM((tm, d), BF16)],
        compiler_params=_cparams(("arbitrary", "arbitrary")),
        name="ffn_dense",
    )(x2, g, mod4, mod4, mod4, wg, wu, wd)


def _pack_bf16_pairs(h):
    w = h.shape[1] // 2
    bits = lax.bitcast_convert_type(h.astype(BF16).astype(F32), jnp.uint32)
    return (bits[:, :w] >> 16) | (bits[:, w:] & jnp.uint32(0xFFFF0000))


def _unpack_bf16_pairs(u):
    lo = lax.bitcast_convert_type(u << 16, F32)
    hi = lax.bitcast_convert_type(u & jnp.uint32(0xFFFF0000), F32)
    return jnp.concatenate([lo, hi], axis=1).astype(BF16)


def _router_kernel(x_ref, g_ref, sc_ref, sh_ref, wh_ref, tri_ref, hp_ref, meta_ref, cnt_ref,
                   h_ref, run_ref):
    @pl.when(pl.program_id(0) == 0)
    def _():
        run_ref[...] = jnp.zeros_like(run_ref)

    g, sc, sh = g_ref[...], sc_ref[...], sh_ref[...]

    def norm_rows(r, carry):
        rows = pl.ds(pl.multiple_of(r * NORM_ROWS, NORM_ROWS), NORM_ROWS)
        hr = _norm_mod(x_ref[rows, :], g, sc, sh)
        h_ref[rows, :] = hr
        hp_ref[rows, :] = _pack_bf16_pairs(hr)
        return carry

    lax.fori_loop(0, x_ref.shape[0] // NORM_ROWS, norm_rows, 0, unroll=8)
    h = h_ref[...]
    h_hi = h.astype(BF16)
    h_lo = (h - h_hi.astype(F32)).astype(BF16)
    both = _dot(h_hi, wh_ref[...])
    logits = both[:, :LANES] + both[:, LANES:] + _dot(h_lo, wh_ref[:, :LANES])
    lane = lax.broadcasted_iota(I32, logits.shape, 1).astype(F32)
    l1 = jnp.where(lane < N_EXPERTS, logits, -jnp.inf)
    m1 = jnp.max(l1, axis=1, keepdims=True)
    i1 = jnp.min(jnp.where(l1 == m1, lane, float(LANES)), axis=1, keepdims=True)
    l2 = jnp.where(lane == i1, -jnp.inf, l1)
    m2 = jnp.max(l2, axis=1, keepdims=True)
    i2 = jnp.min(jnp.where(l2 == m2, lane, float(LANES)), axis=1, keepdims=True)
    e2 = jnp.exp(m2 - m1)
    w1 = 1.0 / (1.0 + e2)
    w2 = e2 / (1.0 + e2)
    member = jnp.where((lane == i1) | (lane == i2), 1.0, 0.0)
    run = run_ref[0:1, :]
    rank = _dot(tri_ref[...], member.astype(BF16)) + run
    p1 = jnp.sum(jnp.where(lane == i1, rank, 0.0), axis=1, keepdims=True)
    p2 = jnp.sum(jnp.where(lane == i2, rank, 0.0), axis=1, keepdims=True)
    run = run + jnp.sum(member, axis=0, keepdims=True)
    run_ref[0:1, :] = run
    cnt_ref[...] = jnp.broadcast_to(run, cnt_ref.shape)
    meta = jnp.zeros(logits.shape, F32)
    for idx, val in enumerate([i1, i2, w1, w2, p1, p2]):
        meta = jnp.where(lane == float(idx), val, meta)
    meta_ref[...] = meta


def _router(x2, g, mod4, wr_hi, wr_lo, seq):
    n, d = x2.shape
    tm = 512
    per_b = seq // tm
    tri = jnp.asarray(np.tril(np.ones((tm, tm), np.float32), -1), BF16)
    mod_spec = lambda k: pl.BlockSpec((None, None, 1, d), lambda i: (i // per_b, k, 0, 0))
    return pl.pallas_call(
        _router_kernel,
        grid=(n // tm,),
        in_specs=[
            pl.BlockSpec((tm, d), lambda i: (i, 0)),
            pl.BlockSpec((1, d), lambda i: (0, 0)),
            mod_spec(4), mod_spec(3),
            pl.BlockSpec((d, 2 * LANES), lambda i: (0, 0)),
            pl.BlockSpec((tm, tm), lambda i: (0, 0)),
        ],
        out_specs=[
            pl.BlockSpec((tm, d // 2), lambda i: (i, 0)),
            pl.BlockSpec((tm, LANES), lambda i: (i, 0)),
            pl.BlockSpec((8, LANES), lambda i: (0, 0)),
        ],
        out_shape=[jax.ShapeDtypeStruct((n, d // 2), jnp.uint32),
                   jax.ShapeDtypeStruct((n, LANES), F32),
                   jax.ShapeDtypeStruct((8, LANES), F32)],
        scratch_shapes=[pltpu.VMEM((tm, d), F32), pltpu.VMEM((8, LANES), F32)],
        compiler_params=_cparams(("arbitrary",)),
        name="router",
    )(x2, g, mod4, mod4, jnp.concatenate([wr_hi, wr_lo], axis=1), tri)


MOE_TM = 1024
DISP_TM = 512


def _dispatch_kernel(dest_hbm, hp_ref, xs_in_hbm, xs_hbm, idx_ref, sem_i, sem_r):
    del xs_in_hbm
    i = pl.program_id(0)
    n_idx = 2 * DISP_TM
    cp = pltpu.make_async_copy(dest_hbm.at[pl.ds(pl.multiple_of(i * n_idx, n_idx), n_idx)], idx_ref, sem_i)
    cp.start()
    cp.wait()

    def row_copy(r, k):
        return pltpu.make_async_copy(hp_ref.at[pl.ds(r, 1), :],
                                     xs_hbm.at[pl.ds(idx_ref[k * DISP_TM + r], 1), :], sem_r)

    def start(r, c):
        row_copy(r, 0).start()
        row_copy(r, 1).start()
        return c

    def wait(r, c):
        row_copy(r, 0).wait()
        row_copy(r, 1).wait()
        return c

    lax.fori_loop(0, DISP_TM, start, 0, unroll=8)
    lax.fori_loop(0, DISP_TM, wait, 0, unroll=8)


def _dispatch(dest_tiles, hp, n_rows):
    n, w = hp.shape
    xs0 = jnp.zeros((n_rows, w), jnp.uint32)
    return pl.pallas_call(
        _dispatch_kernel,
        grid=(n // DISP_TM,),
        in_specs=[
            pl.BlockSpec(memory_space=pl.ANY),
            pl.BlockSpec((DISP_TM, w), lambda i: (i, 0)),
            pl.BlockSpec(memory_space=pl.ANY),
        ],
        out_specs=pl.BlockSpec(memory_space=pl.ANY),
        out_shape=jax.ShapeDtypeStruct((n_rows, w), jnp.uint32),
        scratch_shapes=[
            pltpu.SMEM((2 * DISP_TM,), I32),
            pltpu.SemaphoreType.DMA(()),
            pltpu.SemaphoreType.DMA(()),
        ],
        input_output_aliases={2: 0},
        compiler_params=_cparams(("arbitrary",)),
        name="moe_dispatch",
    )(dest_tiles, hp, xs0)


MOE_SUB = 256


def _moe_kernel(te_ref, nu_ref, tr_ref, xs_ref, wg_ref, wu_ref, wd_ref, o_ref, xb_ref):
    t = pl.program_id(0)
    f = pl.program_id(1)
    rows = tr_ref[t]

    @pl.when(f == 0)
    def _():
        o_ref[...] = jnp.zeros_like(o_ref)

    @pl.when((rows > 0) & (f == 0))
    def _():
        xb_ref[...] = _unpack_bf16_pairs(xs_ref[...])

    @pl.when(rows == MOE_TM)
    def _():
        o_ref[...] += _swiglu_tile(xb_ref[...], wg_ref, wu_ref, wd_ref)

    for sb in range(MOE_TM // MOE_SUB):
        @pl.when((rows < MOE_TM) & (rows > sb * MOE_SUB))
        def _(sb=sb):
            sl = slice(sb * MOE_SUB, (sb + 1) * MOE_SUB)
            o_ref[sl, :] += _swiglu_tile(xb_ref[sl, :], wg_ref, wu_ref, wd_ref)


def _moe(tile_expert, n_used, tile_rows, xs, wg, wu, wd):
    d = wg.shape[1]
    n_tiles = tile_expert.shape[0]
    dff = wg.shape[2]
    tf = 512
    nf = dff // tf

    def f_blk(t, f, nu):
        return jnp.where(t < nu[0], f, nf - 1)

    grid_spec = pltpu.PrefetchScalarGridSpec(
        num_scalar_prefetch=3,
        grid=(n_tiles, nf),
        in_specs=[
            pl.BlockSpec((MOE_TM, d // 2), lambda t, f, te, nu, tr: (jnp.minimum(t, nu[0] - 1), 0)),
            pl.BlockSpec((None, d, tf), lambda t, f, te, nu, tr: (te[t], 0, f_blk(t, f, nu))),
            pl.BlockSpec((None, d, tf), lambda t, f, te, nu, tr: (te[t], 0, f_blk(t, f, nu))),
            pl.BlockSpec((None, tf, d), lambda t, f, te, nu, tr: (te[t], f_blk(t, f, nu), 0)),
        ],
        out_specs=pl.BlockSpec((MOE_TM, d), lambda t, f, te, nu, tr: (t, 0)),
        scratch_shapes=[pltpu.VMEM((MOE_TM, d), BF16)],
    )
    return pl.pallas_call(
        _moe_kernel,
        grid_spec=grid_spec,
        out_shape=jax.ShapeDtypeStruct((n_tiles * MOE_TM, d), F32),
        compiler_params=_cparams(("arbitrary", "arbitrary")),
        name="moe_experts",
    )(tile_expert, n_used, tile_rows, xs, wg, wu, wd)


COMB_TM = 512


def _combine_kernel(dest_hbm, y_hbm, x_ref, meta_ref, gt_ref, o_ref, idx_ref, y_ref, sem_i, sem_r):
    i = pl.program_id(0)
    n_idx = 2 * COMB_TM

    def row_copy(slot, j):
        return pltpu.make_async_copy(
            y_hbm.at[pl.ds(idx_ref[slot * n_idx + j], 1), :],
            y_ref.at[slot, pl.ds(j, 1), :], sem_r.at[slot])

    def fetch(step, slot):
        cp = pltpu.make_async_copy(
            dest_hbm.at[pl.ds(pl.multiple_of(step * n_idx, n_idx), n_idx)],
            idx_ref.at[pl.ds(pl.multiple_of(slot * n_idx, n_idx), n_idx)], sem_i)
        cp.start()
        cp.wait()

        def start(j, c):
            row_copy(slot, j).start()
            return c

        lax.fori_loop(0, n_idx, start, 0, unroll=8)

    @pl.when(i == 0)
    def _():
        fetch(i, 0)

    @pl.when(i + 1 < pl.num_programs(0))
    def _():
        fetch(i + 1, (i + 1) % 2)

    slot = i % 2

    def wait(j, c):
        row_copy(slot, j).wait()
        return c

    lax.fori_loop(0, n_idx, wait, 0, unroll=8)
    w1 = meta_ref[:, 2:3]
    w2 = meta_ref[:, 3:4]
    y1 = y_ref[slot, 0:COMB_TM, :]
    y2 = y_ref[slot, COMB_TM:n_idx, :]
    o_ref[...] = x_ref[...] + gt_ref[...] * (w1 * y1 + w2 * y2)


def _combine(dest, y_sorted, x2, meta, mod4, seq):
    n, d = x2.shape
    tm = COMB_TM
    per_b = seq // tm
    return pl.pallas_call(
        _combine_kernel,
        grid=(n // tm,),
        in_specs=[
            pl.BlockSpec(memory_space=pl.ANY),
            pl.BlockSpec(memory_space=pl.ANY),
            pl.BlockSpec((tm, d), lambda i: (i, 0)),
            pl.BlockSpec((tm, LANES), lambda i: (i, 0)),
            pl.BlockSpec((None, None, 1, d), lambda i: (i // per_b, 5, 0, 0)),
        ],
        out_specs=pl.BlockSpec((tm, d), lambda i: (i, 0)),
        out_shape=jax.ShapeDtypeStruct((n, d), F32),
        scratch_shapes=[
            pltpu.SMEM((2 * 2 * tm,), I32),
            pltpu.VMEM((2, 2 * tm, d), F32),
            pltpu.SemaphoreType.DMA(()),
            pltpu.SemaphoreType.DMA((2,)),
        ],
        compiler_params=_cparams(("arbitrary",)),
        name="moe_combine",
    )(dest, y_sorted, x2, meta, mod4)


def _moe_layer(x2, g, mod4, w_router, wg, wu, wd, seq):
    n, d = x2.shape
    wr = jnp.zeros((d, LANES), F32).at[:, :N_EXPERTS].set(w_router)
    wr_hi = wr.astype(BF16)
    wr_lo = (wr - wr_hi.astype(F32)).astype(BF16)
    hp, meta, cnt = _router(x2, g, mod4, wr_hi, wr_lo, seq)

    n_tiles = 2 * n // MOE_TM + N_EXPERTS
    counts = cnt[0, :N_EXPERTS].astype(I32)
    tiles_per = (counts + MOE_TM - 1) // MOE_TM
    tile_end = jnp.cumsum(tiles_per)
    row_start = (tile_end - tiles_per) * MOE_TM
    n_used = tile_end[-1]
    idx = meta[:, 0:2].astype(I32)
    pos = meta[:, 4:6].astype(I32)
    start_of = sum(jnp.where(idx == e, row_start[e], 0) for e in range(N_EXPERTS))
    dest = start_of + pos
    all_tiles = jnp.arange(n_tiles, dtype=I32)
    tile_ids = jnp.minimum(all_tiles, n_used - 1)
    tile_expert = jnp.sum(tile_ids[:, None] >= tile_end[None, :], axis=1).astype(I32)
    onehot_e = tile_expert[:, None] == jnp.arange(N_EXPERTS, dtype=I32)[None, :]
    first_tile = jnp.sum(jnp.where(onehot_e, (tile_end - tiles_per)[None, :], 0), axis=1)
    cnt_tile = jnp.sum(jnp.where(onehot_e, counts[None, :], 0), axis=1)
    tile_rows = jnp.clip(cnt_tile - (all_tiles - first_tile) * MOE_TM, 0, MOE_TM)
    tile_rows = jnp.where(all_tiles < n_used, tile_rows, 0).astype(I32)
    assert COMB_TM == DISP_TM
    dest_tiles = dest.reshape(n // COMB_TM, COMB_TM, 2).transpose(0, 2, 1).reshape(-1)

    xs = _dispatch(dest_tiles, hp, n_tiles * MOE_TM)
    y_sorted = _moe(tile_expert, n_used.reshape(1).astype(I32), tile_rows, xs, wg, wu, wd)
    return _combine(dest_tiles, y_sorted, x2, meta, mod4, seq)


def _t5_bucket(dist):
    max_exact = N_BUCKETS // 2
    d = np.maximum(dist, 1).astype(np.float32)
    large = max_exact + (np.log(d / max_exact) / np.log(MAX_DIST / max_exact)
                         * (N_BUCKETS - max_exact)).astype(np.int32)
    large = np.minimum(large, N_BUCKETS - 1)
    return np.where(dist < max_exact, dist, large).astype(np.int32)


def _pack_w_in(w):
    splits = [M_HEADS * M_QK, M_HEADS * M_QK, M_HEADS * M_V, M_HEADS * M_V, 2 * M_HEADS,
              G_HEADS * G_QK, G_HEADS * G_QK, G_HEADS * G_V, G_HEADS * G_V, G_RANK,
              A_HEADS * A_HD, A_KV * A_HD, A_KV * A_HD]
    offs = np.concatenate([[0], np.cumsum(splits)])
    seg = lambda i: w[:, offs[i]:offs[i + 1]]
    main = jnp.concatenate([seg(i) for i in (0, 1, 2, 3, 5, 6, 7, 8, 10, 11, 12)], axis=1)
    small = jnp.concatenate(
        [seg(4), seg(9), jnp.zeros((w.shape[0], LANES - 2 * M_HEADS - G_RANK), w.dtype)], axis=1)
    return main.astype(BF16), small.astype(BF16)


def kernel(x, c, w_ada, b_ada, g_mix_norm, g_ffn_norm, w_in, b_gates_m, w_conv_m, b_conv_m, g_out_m,
           w_gla_decay, b_gla_decay, g_out_g, g_qnorm, g_knorm, sinks, rel_bias, w_out, w_ffn_gate,
           w_ffn_up, w_ffn_down, w_router, w_moe_gate, w_moe_up, w_moe_down):
    bsz, seq, d = x.shape
    depth = w_ada.shape[0]
    n = bsz * seq
    assert seq % 1024 == 0 and d == 2048 and n % MOE_TM == 0 and bsz % max(MLSTM_NB, GLA_NB, SWA_NB) == 0

    mod = _ada(c, w_ada, b_ada)

    jj = np.arange(BLK)[:, None]
    ss = np.arange(2 * BLK)[None, :]
    buckets = _t5_bucket(np.clip(jj + BLK - ss, 0, None))
    onehot = jnp.asarray(np.eye(N_BUCKETS, dtype=np.float32)[buckets.reshape(-1)])
    bias = jnp.einsum("pb,bh->hp", onehot, rel_bias.astype(F32),
                      precision=lax.Precision.HIGHEST).reshape(A_HEADS, BLK, 2 * BLK)
    in_window = (jj + BLK - ss >= 0) & (jj + BLK - ss < WINDOW)
    masks = np.stack([in_window & (ss >= BLK), in_window])
    bias = jnp.where(jnp.asarray(masks)[:, None], (bias * LOG2E)[None], -jnp.inf)
    ind = jnp.asarray(np.kron(np.eye(LANES // A_HD), np.ones((A_HD, A_HD))) / A_HD, BF16)

    x2 = x.reshape(n, d)
    for l in range(depth):
        mod4 = mod[l].reshape(bsz, 6, 1, d)
        w_main, w_small = _pack_w_in(w_in[l])
        pm, ps = _inproj(x2, g_mix_norm[l].reshape(1, d), mod4, w_main, w_small, seq)
        pm = pm.reshape(bsz, seq, -1)
        ps = ps.reshape(bsz, seq, -1)

        gb = jnp.zeros((1, LANES), F32).at[0, :2 * M_HEADS].set(b_gates_m[l].reshape(-1))
        ym = _mlstm(pm, ps, w_conv_m[l], b_conv_m[l].reshape(1, -1), gb, g_out_m[l].reshape(1, -1))
        wdec = jnp.zeros((LANES, G_HEADS * G_QK), F32).at[S_GA:S_GA + G_RANK].set(w_gla_decay[l])
        yg = _gla(pm, ps, wdec, b_gla_decay[l].reshape(1, -1), g_out_g[l].reshape(1, -1))
        gq = jnp.tile(g_qnorm[l], A_HEADS).reshape(1, -1) * (A_HD ** -0.5 * LOG2E)
        gk = jnp.tile(g_knorm[l], A_KV).reshape(1, -1)
        ya = _swa(pm, bias, gq, gk, sinks[l] * LOG2E, ind)
        x2 = _outproj(ym.reshape(n, -1), yg.reshape(n, -1), ya.reshape(n, -1), w_out[l].astype(BF16),
                      x2, mod4, seq)

        gf = g_ffn_norm[l].reshape(1, d)
        if l % 2 == 0:
            j = l // 2
            x2 = _ffn(x2, gf, mod4, w_ffn_gate[j], w_ffn_up[j], w_ffn_down[j], seq)
        else:
            j = l // 2
            x2 = _moe_layer(x2, gf, mod4, w_router[j], w_moe_gate[j], w_moe_up[j], w_moe_down[j], seq)
    return x2.reshape(bsz, seq, d)
```

```python
import functools

import numpy as np
import jax
import jax.numpy as jnp
from jax import lax
from jax.experimental import pallas as pl
from jax.experimental.pallas import tpu as pltpu

F32 = jnp.float32
BF16 = jnp.bfloat16
I32 = jnp.int32

M_HEADS = 4
M_QK = 64
M_V = 128
CONV_W = 4
G_HEADS = 4
G_QK = 64
G_V = 128
G_RANK = 16
G_TAU = 16.0
G_CHUNK = 64
G_SUB = 16
A_HD = 64
A_HEADS = 16
A_KV = 2
WINDOW = 128
N_BUCKETS = 32
MAX_DIST = 128
N_EXPERTS = 8
EPS = 1e-6
LOG2E = 1.4426950408889634

LANES = 128
BLK = 128
VMEM_LIMIT = 60000 * 1024

C_MQK, C_MV, C_MO = 0, 512, 1024
C_GQK, C_GV, C_GG = 1536, 2048, 2560
C_AQ, C_AK, C_AV = 3072, 4096, 4224
N_MAIN = 4352
S_GA = 8


def _cparams(sem):
    return pltpu.CompilerParams(dimension_semantics=sem, vmem_limit_bytes=VMEM_LIMIT)


def _dot(a, b):
    return jnp.dot(a, b, preferred_element_type=F32)


def _dot_nt(a, b):
    return lax.dot_general(a, b, (((1,), (1,)), ((), ())), preferred_element_type=F32)


def _split_dot(a, b_f32):
    hi = b_f32.astype(BF16)
    lo = (b_f32 - hi.astype(F32)).astype(BF16)
    return _dot(a, hi) + _dot(a, lo)


def _sigmoid(x):
    return 1.0 / (1.0 + jnp.exp(-x))


def _log_sigmoid(x):
    return jnp.minimum(x, 0.0) - jnp.log(1.0 + jnp.exp(-jnp.abs(x)))


def _norm_mod(x, g, sc, sh):
    ms = jnp.mean(x * x, axis=-1, keepdims=True)
    return (x * lax.rsqrt(ms + EPS) * g) * (1.0 + sc) + sh


NORM_ROWS = 16


def _norm_mod_rows(x_ref, g_ref, sc_ref, sh_ref, out_refs):
    g, sc, sh = g_ref[...], sc_ref[...], sh_ref[...]

    def body(r, carry):
        rows = pl.ds(pl.multiple_of(r * NORM_ROWS, NORM_ROWS), NORM_ROWS)
        h = _norm_mod(x_ref[rows, :], g, sc, sh)
        for ref in out_refs:
            ref[rows, :] = h.astype(ref.dtype)
        return carry

    lax.fori_loop(0, x_ref.shape[0] // NORM_ROWS, body, 0, unroll=8)


def _ada_kernel(c_ref, w_ref, b_ref, o_ref):
    c = c_ref[...]
    cond = (c * _sigmoid(c)).astype(BF16)
    o_ref[...] = _dot(cond, w_ref[...].astype(BF16)) + b_ref[...]


def _ada(c, w_ada, b_ada):
    depth, d, n6 = w_ada.shape
    bsz = c.shape[0]
    tn = 1024
    return pl.pallas_call(
        _ada_kernel,
        grid=(depth, n6 // tn),
        in_specs=[
            pl.BlockSpec((bsz, d), lambda l, n: (0, 0)),
            pl.BlockSpec((None, d, tn), lambda l, n: (l, 0, n)),
            pl.BlockSpec((None, 1, tn), lambda l, n: (l, 0, n)),
        ],
        out_specs=pl.BlockSpec((None, bsz, tn), lambda l, n: (l, 0, n)),
        out_shape=jax.ShapeDtypeStruct((depth, bsz, n6), F32),
        compiler_params=_cparams(("arbitrary", "arbitrary")),
        name="ada_mod",
    )(c, w_ada, b_ada.reshape(depth, 1, n6))


def _inproj_kernel(x_ref, g_ref, sc_ref, sh_ref, wm_ref, ws_ref, om_ref, os_ref, h_ref):
    @pl.when(pl.program_id(1) == 0)
    def _():
        _norm_mod_rows(x_ref, g_ref, sc_ref, sh_ref, [h_ref])
        os_ref[...] = _dot(h_ref[...], ws_ref[...])

    om_ref[...] = _dot(h_ref[...], wm_ref[...]).astype(BF16)


def _inproj(x2, g, mod4, w_main, w_small, seq):
    n, d = x2.shape
    tm, tn = 1024, N_MAIN // 2
    per_b = seq // tm
    mod_spec = lambda k: pl.BlockSpec((None, None, 1, d), lambda i, j: (i // per_b, k, 0, 0))
    return pl.pallas_call(
        _inproj_kernel,
        grid=(n // tm, N_MAIN // tn),
        in_specs=[
            pl.BlockSpec((tm, d), lambda i, j: (i, 0)),
            pl.BlockSpec((1, d), lambda i, j: (0, 0)),
            mod_spec(1), mod_spec(0),
            pl.BlockSpec((d, tn), lambda i, j: (0, j)),
            pl.BlockSpec((d, LANES), lambda i, j: (0, 0)),
        ],
        out_specs=[
            pl.BlockSpec((tm, tn), lambda i, j: (i, j)),
            pl.BlockSpec((tm, LANES), lambda i, j: (i, 0)),
        ],
        out_shape=[jax.ShapeDtypeStruct((n, N_MAIN), BF16),
                   jax.ShapeDtypeStruct((n, LANES), F32)],
        scratch_shapes=[pltpu.VMEM((tm, d), BF16)],
        compiler_params=_cparams(("arbitrary", "arbitrary")),
        name="in_proj",
    )(x2, g, mod4, mod4, w_main, w_small)


def _per_batch(body, n_batch_refs, n_state_refs):
    def kern(*refs):
        n_shared = len(refs) - n_batch_refs - n_state_refs - 1
        batch_in, shared = refs[:n_batch_refs], refs[n_batch_refs:n_batch_refs + n_shared]
        y_ref = refs[n_batch_refs + n_shared]
        state = refs[len(refs) - n_state_refs:]

        @pl.when(pl.program_id(1) == 0)
        def _():
            for st in state:
                st[...] = jnp.zeros_like(st)

        for bb in range(y_ref.shape[0]):
            body(*[r.at[bb] for r in batch_in], *shared, y_ref.at[bb], *[st.at[bb] for st in state])
    return kern


def _mlstm_body(qk_ref, v_ref, o_ref, sm_ref, wc_ref, bc_ref, gb_ref, go_ref, y_ref,
                tail_ref, ct_ref, n_ref, m_ref):
    L = BLK
    nqk = M_HEADS * M_QK

    x = qk_ref[...].astype(F32)
    prev = tail_ref[...]
    row8 = lax.broadcasted_iota(I32, prev.shape, 0)
    acc = x * wc_ref[CONV_W - 1:CONV_W, :] + bc_ref[...]
    for dlt in range(1, CONV_W):
        xs = pltpu.roll(x, dlt, axis=0)
        ps = pltpu.roll(prev, dlt, axis=0)
        top = jnp.where(row8 < dlt, ps, xs[0:8])
        xs = jnp.concatenate([top, xs[8:]], axis=0)
        acc = acc + xs * wc_ref[CONV_W - 1 - dlt:CONV_W - dlt, :]
    tail_ref[...] = x[L - 8:L]
    qk = acc * _sigmoid(acc)
    q = qk[:, :nqk]
    k = qk[:, nqk:] * (M_QK ** -0.5)
    kt_b = k.T.astype(BF16)

    gates = sm_ref[...] + gb_ref[...]
    row = lax.broadcasted_iota(I32, (L, L), 0)
    col = lax.broadcasted_iota(I32, (L, L), 1)
    causal = row >= col
    tri = causal.astype(BF16)
    bcum = _split_dot(tri, _log_sigmoid(gates))
    z = jnp.where(col < M_HEADS, gates, bcum)
    zt = z.T
    lane_q = lax.broadcasted_iota(I32, (L, nqk), 1)

    n_all = n_ref[...]
    head_n = lax.broadcasted_iota(I32, n_all.shape, 1) // M_QK
    row_n = lax.broadcasted_iota(I32, n_all.shape, 0)
    qn = _dot_nt(q.astype(BF16), jnp.where(head_n == row_n, n_all, 0.0).astype(BF16))
    lane_w = lax.broadcasted_iota(I32, (L, LANES), 1)
    w_cols = jnp.zeros((L, LANES), F32)
    decay_rows = jnp.zeros((n_all.shape[0], 1), F32)

    outs = []
    for h in range(M_HEADS):
        in_head = (lane_q >= h * M_QK) & (lane_q < (h + 1) * M_QK)
        qm = jnp.where(in_head, q, 0.0)
        qm_b = qm.astype(BF16)
        v_h = v_ref[:, h * M_V:(h + 1) * M_V]
        m_prev = m_ref[h:h + 1, 0:1]
        ct_h = ct_ref[h]
        bcol = z[:, M_HEADS + h:M_HEADS + h + 1]
        icol = z[:, h:h + 1]
        brow = zt[M_HEADS + h:M_HEADS + h + 1, :]
        irow = zt[h:h + 1, :]
        logd = jnp.where(causal, bcol - brow + irow, -jnp.inf)
        log_inter = bcol + m_prev
        m_row = jnp.maximum(log_inter, jnp.max(logd, axis=1, keepdims=True))
        dmat = jnp.exp(logd - m_row)
        w_inter = jnp.exp(log_inter - m_row)
        s = _dot(qm_b, kt_b) * dmat
        num = w_inter * _dot(qm_b, ct_h.astype(BF16)) + _dot(s.astype(BF16), v_h)
        den = w_inter * qn[:, h:h + 1] + jnp.sum(s, axis=1, keepdims=True)
        outs.append(num / jnp.maximum(jnp.abs(den), jnp.exp(-m_row)))
        b_last = bcol[L - 1:L, :]
        log_w = b_last - bcol + icol
        m_new = jnp.maximum(b_last + m_prev, jnp.max(log_w, axis=0, keepdims=True))
        w = jnp.exp(log_w - m_new)
        decay = jnp.exp(b_last + m_prev - m_new)
        vw = (v_h.astype(F32) * w).astype(BF16)
        ct_ref[h] = decay * ct_h + _dot(kt_b, vw)
        w_cols = jnp.where(lane_w == h, w, w_cols)
        decay_rows = jnp.where(row_n[:, 0:1] == h, decay, decay_rows)
        m_ref[h:h + 1, :] = jnp.broadcast_to(m_new, (1, LANES))

    n_upd = _dot(w_cols.T[0:n_all.shape[0]].astype(BF16), k.astype(BF16))
    n_ref[...] = decay_rows * n_all + n_upd

    og = _sigmoid(o_ref[...].astype(F32))
    for h in range(M_HEADS):
        hh = outs[h]
        ms = jnp.mean(hh * hh, axis=1, keepdims=True)
        sl = slice(h * M_V, (h + 1) * M_V)
        y = hh * lax.rsqrt(ms + EPS) * go_ref[:, sl]
        y_ref[:, sl] = (y * og[:, sl]).astype(BF16)


MLSTM_NB, GLA_NB, SWA_NB = 1, 2, 2


def _mixer_specs(bsz, seq, nb):
    rows = lambda width, cb: pl.BlockSpec((nb, BLK, width), lambda b, c: (b, c, cb))
    full = lambda shp: pl.BlockSpec(shp, lambda b, c: (0,) * len(shp))
    return rows, full, (bsz // nb, seq // BLK)


def _mlstm(pm, ps, wc, bc, gb, go):
    bsz, seq, _ = pm.shape
    wd = M_HEADS * M_V
    MIX_NB = MLSTM_NB
    rows, full, grid = _mixer_specs(bsz, seq, MIX_NB)
    return pl.pallas_call(
        _per_batch(_mlstm_body, 4, 4),
        grid=grid,
        in_specs=[
            rows(wd, C_MQK // wd), rows(wd, C_MV // wd), rows(wd, C_MO // wd), rows(LANES, 0),
            full((CONV_W, 2 * M_HEADS * M_QK)), full((1, 2 * M_HEADS * M_QK)),
            full((1, LANES)), full((1, wd)),
        ],
        out_specs=rows(wd, 0),
        out_shape=jax.ShapeDtypeStruct((bsz, seq, wd), BF16),
        scratch_shapes=[
            pltpu.VMEM((MIX_NB, 8, 2 * M_HEADS * M_QK), F32),
            pltpu.VMEM((MIX_NB, M_HEADS, M_HEADS * M_QK, M_V), F32),
            pltpu.VMEM((MIX_NB, 8, M_HEADS * M_QK), F32),
            pltpu.VMEM((MIX_NB, 8, LANES), F32),
        ],
        compiler_params=_cparams(("arbitrary", "arbitrary")),
        name="mlstm",
    )(pm, pm, pm, ps, wc, bc, gb, go)


def _gla_body(qk_ref, v_ref, g_ref, sm_ref, wd_ref, bd_ref, go_ref, y_ref, s_ref):
    L, C, SC = BLK, G_CHUNK, G_SUB
    nqk = G_HEADS * G_QK
    pair_k = 2 * G_QK
    pair_v = 2 * G_V
    n_pairs = G_HEADS // 2

    zdec = _dot(sm_ref[...].astype(BF16), wd_ref[...].astype(BF16)) + bd_ref[...]
    log_a = _log_sigmoid(zdec) * (LOG2E / G_TAU)
    row = lax.broadcasted_iota(I32, (L, L), 0)
    col = lax.broadcasted_iota(I32, (L, L), 1)
    same_chunk = (row // C) == (col // C)
    tri = ((row >= col) & same_chunk).astype(BF16)
    b = _split_dot(tri, log_a)
    bt = b.T

    q = qk_ref[:, :nqk].astype(F32) * (G_QK ** -0.5)
    k = qk_ref[:, nqk:].astype(F32)
    b_end = jnp.concatenate(
        [jnp.broadcast_to(b[(cc + 1) * C - 1:(cc + 1) * C, :], (C, nqk)) for cc in range(L // C)], axis=0)
    kdt = (k * jnp.exp2(b_end - b)).T
    qe = q * jnp.exp2(b)

    dloc = col - (row // SC) * SC
    diag_sel = jnp.where((dloc >= 0) & (dloc <= row % SC), dloc, -1)
    sub_r = (row % C) // SC
    off_sel = jnp.where(same_chunk & ((col % C) // SC < sub_r), sub_r, -1)
    diag_sel2 = jnp.concatenate([diag_sel, diag_sel], axis=1)
    off_sel2 = jnp.concatenate([off_sel, off_sel], axis=1)

    lane_s = lax.broadcasted_iota(I32, (pair_k, L), 1)
    head_k = lax.broadcasted_iota(I32, (L, pair_k), 1) // G_QK
    rblk = lax.broadcasted_iota(I32, (pair_k, pair_v), 0) // G_QK
    cblk = lax.broadcasted_iota(I32, (pair_k, pair_v), 1) // G_V
    blockdiag = rblk == cblk
    vrow = lax.broadcasted_iota(I32, (2 * L, pair_v), 0) // L
    vcol = lax.broadcasted_iota(I32, (2 * L, pair_v), 1) // G_V
    vdiag = vrow == vcol

    def stack_heads(x):
        return jnp.concatenate([jnp.where(head_k == hh, x, 0.0) for hh in range(2)], axis=0).astype(BF16)

    pair_out = []
    for p in range(n_pairs):
        kl = slice(p * pair_k, (p + 1) * pair_k)
        vl = slice(p * pair_v, (p + 1) * pair_v)
        bp, qp, kp = b[:, kl], q[:, kl], k[:, kl]
        v_p = v_ref[:, vl]
        b3 = bp.reshape(L // SC, SC, pair_k)

        def sub_row(i, b3=b3):
            return jnp.broadcast_to(b3[:, i:i + 1, :], b3.shape).reshape(L, pair_k)

        kst = stack_heads(kp)
        att = jnp.zeros((L, 2 * L), F32)
        for i in range(SC):
            e = jnp.exp2(jnp.minimum(bp - sub_row(i), 0.0))
            r = _dot_nt((qp * e).astype(BF16), kst)
            att = jnp.where(diag_sel2 == i, r, att)

        qs = (qp * jnp.exp2(bp - sub_row(0))).astype(BF16)
        for sub in range(1, C // SC):
            bref = jnp.concatenate(
                [jnp.broadcast_to(bp[cc * C + sub * SC:cc * C + sub * SC + 1, :], (C, pair_k))
                 for cc in range(L // C)], axis=0)
            ks = kp * jnp.exp2(jnp.minimum(bref - bp, 0.0))
            att = jnp.where(off_sel2 == sub, _dot_nt(qs, stack_heads(ks)), att)

        vblk = jnp.where(vdiag, jnp.concatenate([v_p, v_p], axis=0), 0)
        intra = _dot(att.astype(BF16), vblk)

        outs = []
        for cc in range(L // C):
            r0 = cc * C
            s_p = s_ref[p]
            outs.append(_dot(qe[r0:r0 + C, kl].astype(BF16), s_p.astype(BF16)) + intra[r0:r0 + C])
            in_chunk = (lane_s >= r0) & (lane_s < r0 + C)
            kd = jnp.where(in_chunk, kdt[kl, :], 0.0).astype(BF16)
            upd = jnp.where(blockdiag, _dot(kd, v_p), 0.0)
            dcol = jnp.exp2(bt[kl, r0 + C - 1:r0 + C])
            s_ref[p] = dcol * s_p + upd
        pair_out.append(outs)

    gate = g_ref[...].astype(F32)
    gate = gate * _sigmoid(gate)
    for p in range(n_pairs):
        o_p = jnp.concatenate(pair_out[p], axis=0)
        for hh in range(2):
            h = 2 * p + hh
            oh = o_p[:, hh * G_V:(hh + 1) * G_V]
            ms = jnp.mean(oh * oh, axis=1, keepdims=True)
            sl = slice(h * G_V, (h + 1) * G_V)
            y = oh * lax.rsqrt(ms + EPS) * go_ref[:, sl]
            y_ref[:, sl] = (y * gate[:, sl]).astype(BF16)


def _gla(pm, ps, wd, bd, go):
    bsz, seq, _ = pm.shape
    w = G_HEADS * G_V
    nqk = G_HEADS * G_QK
    MIX_NB = GLA_NB
    rows, full, grid = _mixer_specs(bsz, seq, MIX_NB)
    return pl.pallas_call(
        _per_batch(_gla_body, 4, 1),
        grid=grid,
        in_specs=[
            rows(w, C_GQK // w), rows(w, C_GV // w), rows(w, C_GG // w), rows(LANES, 0),
            full((LANES, nqk)), full((1, nqk)), full((1, w)),
        ],
        out_specs=rows(w, 0),
        out_shape=jax.ShapeDtypeStruct((bsz, seq, w), BF16),
        scratch_shapes=[
            pltpu.VMEM((MIX_NB, G_HEADS // 2, 2 * G_QK, 2 * G_V), F32),
        ],
        compiler_params=_cparams(("arbitrary", "arbitrary")),
        name="gla",
    )(pm, pm, pm, ps, wd, bd, go)


def _group_mean_sq(x, ind):
    return _split_dot_rhs(x * x, ind)


def _split_dot_rhs(a_f32, b_bf16):
    hi = a_f32.astype(BF16)
    lo = (a_f32 - hi.astype(F32)).astype(BF16)
    return _dot(hi, b_bf16) + _dot(lo, b_bf16)


def _swa_body(q_ref, kc_ref, kp_ref, vc_ref, vp_ref, bias_ref, gq_ref, gk_ref, sink_ref, ind_ref,
              y_ref):
    T = BLK
    heads_per_tile = LANES // A_HD
    n_tiles = A_HEADS // heads_per_tile
    grp = A_HEADS // A_KV
    ind = ind_ref[...]

    qn = []
    for t in range(n_tiles):
        sl = slice(t * LANES, (t + 1) * LANES)
        qt = q_ref[:, sl].astype(F32)
        ms = _group_mean_sq(qt, ind)
        qn.append((qt * lax.rsqrt(ms + EPS) * gq_ref[:, sl]).astype(BF16))
    kk = jnp.concatenate([kp_ref[...], kc_ref[...]], axis=0).astype(F32)
    kn = kk * lax.rsqrt(_group_mean_sq(kk, ind) + EPS) * gk_ref[...]
    vv = jnp.concatenate([vp_ref[...], vc_ref[...]], axis=0).astype(F32)
    k_sw = pltpu.roll(kn, A_HD, axis=1)
    v_sw = pltpu.roll(vv, A_HD, axis=1)
    half = lax.broadcasted_iota(I32, (2 * T, LANES), 1) // A_HD

    acc = [None] * n_tiles
    for g in range(A_KV):
        for p in range(heads_per_tile):
            ksrc, vsrc = (kn, vv) if g == p else (k_sw, v_sw)
            kz = jnp.where(half == p, ksrc, 0.0).astype(BF16)
            vz = jnp.where(half == p, vsrc, 0.0).astype(BF16)
            for u in range(grp // heads_per_tile):
                h = g * grp + heads_per_tile * u + p
                t = h // heads_per_tile
                sc = _dot_nt(qn[t], kz) + bias_ref[h]
                sink = sink_ref[h]
                m = jnp.maximum(jnp.max(sc, axis=1, keepdims=True), sink)
                pr = jnp.exp2(sc - m)
                den = jnp.sum(pr, axis=1, keepdims=True) + jnp.exp2(sink - m)
                o = _dot(pr.astype(BF16), vz) / den
                acc[t] = o if acc[t] is None else acc[t] + o
    for t in range(n_tiles):
        y_ref[:, t * LANES:(t + 1) * LANES] = acc[t].astype(BF16)


def _swa(pm, bias, gq, gk, sinks, ind):
    bsz, seq, _ = pm.shape
    wq = A_HEADS * A_HD
    MIX_NB = SWA_NB
    rows, full, grid = _mixer_specs(bsz, seq, MIX_NB)
    cur = lambda cb: rows(LANES, cb)
    prv = lambda cb: pl.BlockSpec((MIX_NB, BLK, LANES), lambda b, c: (b, jnp.maximum(c - 1, 0), cb))
    return pl.pallas_call(
        _per_batch(_swa_body, 5, 0),
        grid=grid,
        in_specs=[
            rows(wq, C_AQ // wq),
            cur(C_AK // LANES), prv(C_AK // LANES), cur(C_AV // LANES), prv(C_AV // LANES),
            pl.BlockSpec((None, A_HEADS, BLK, 2 * BLK), lambda b, c: (jnp.minimum(c, 1), 0, 0, 0)),
            full((1, wq)), full((1, LANES)),
            pl.BlockSpec(memory_space=pltpu.SMEM),
            full((LANES, LANES)),
        ],
        out_specs=rows(wq, 0),
        out_shape=jax.ShapeDtypeStruct((bsz, seq, wq), BF16),
        compiler_params=_cparams(("arbitrary", "arbitrary")),
        name="swa",
    )(pm, pm, pm, pm, pm, bias, gq, gk, sinks, ind)


def _outproj_kernel(ym_ref, yg_ref, ya_ref, w_ref, x_ref, gt_ref, o_ref):
    y = jnp.concatenate([ym_ref[...], yg_ref[...], ya_ref[...]], axis=1)
    o_ref[...] = x_ref[...] + gt_ref[...] * _dot(y, w_ref[...])


def _outproj(ym, yg, ya, w, x2, mod4, seq):
    n, d = x2.shape
    tm, tn = 1024, d
    per_b = seq // tm
    return pl.pallas_call(
        _outproj_kernel,
        grid=(n // tm, d // tn),
        in_specs=[
            pl.BlockSpec((tm, ym.shape[1]), lambda i, j: (i, 0)),
            pl.BlockSpec((tm, yg.shape[1]), lambda i, j: (i, 0)),
            pl.BlockSpec((tm, ya.shape[1]), lambda i, j: (i, 0)),
            pl.BlockSpec((d, tn), lambda i, j: (0, j), pipeline_mode=pl.Buffered(1)),
            pl.BlockSpec((tm, tn), lambda i, j: (i, j)),
            pl.BlockSpec((None, None, 1, tn), lambda i, j: (i // per_b, 2, 0, j)),
        ],
        out_specs=pl.BlockSpec((tm, tn), lambda i, j: (i, j)),
        out_shape=jax.ShapeDtypeStruct((n, d), F32),
        compiler_params=_cparams(("arbitrary", "arbitrary")),
        name="out_proj",
    )(ym, yg, ya, w, x2, mod4)


def _swiglu_tile(h, wg_ref, wu_ref, wd_ref):
    gate = _dot(h, wg_ref[...].astype(BF16))
    up = _dot(h, wu_ref[...].astype(BF16))
    a = (gate * _sigmoid(gate) * up).astype(BF16)
    return _dot(a, wd_ref[...].astype(BF16))


def _ffn_kernel(x_ref, g_ref, sc_ref, sh_ref, gt_ref, wg_ref, wu_ref, wd_ref, o_ref, h_ref):
    f = pl.program_id(1)

    @pl.when(f == 0)
    def _():
        _norm_mod_rows(x_ref, g_ref, sc_ref, sh_ref, [h_ref])
        o_ref[...] = jnp.zeros_like(o_ref)

    o_ref[...] += _swiglu_tile(h_ref[...], wg_ref, wu_ref, wd_ref)

    @pl.when(f == pl.num_programs(1) - 1)
    def _():
        o_ref[...] = x_ref[...] + gt_ref[...] * o_ref[...]


def _ffn(x2, g, mod4, wg, wu, wd, seq):
    n, d = x2.shape
    dff = wg.shape[1]
    tm, tf = 1024, 512
    per_b = seq // tm
    mod_spec = lambda k: pl.BlockSpec((None, None, 1, d), lambda i, f: (i // per_b, k, 0, 0))
    return pl.pallas_call(
        _ffn_kernel,
        grid=(n // tm, dff // tf),
        in_specs=[
            pl.BlockSpec((tm, d), lambda i, f: (i, 0), pipeline_mode=pl.Buffered(1)),
            pl.BlockSpec((1, d), lambda i, f: (0, 0)),
            mod_spec(4), mod_spec(3), mod_spec(5),
            pl.BlockSpec((d, tf), lambda i, f: (0, f)),
            pl.BlockSpec((d, tf), lambda i, f: (0, f)),
            pl.BlockSpec((tf, d), lambda i, f: (f, 0)),
        ],
        out_specs=pl.BlockSpec((tm, d), lambda i, f: (i, 0)),
        out_shape=jax.ShapeDtypeStruct((n, d), F32),
        scratch_shapes=[pltpu.VMEM((tm, d), BF16)],
        compiler_params=_cparams(("arbitrary", "arbitrary")),
        name="ffn_dense",
    )(x2, g, mod4, mod4, mod4, wg, wu, wd)


def _pack_bf16_pairs(h):
    w = h.shape[1] // 2
    bits = lax.bitcast_convert_type(h.astype(BF16).astype(F32), jnp.uint32)
    return (bits[:, :w] >> 16) | (bits[:, w:] & jnp.uint32(0xFFFF0000))


def _unpack_bf16_pairs(u):
    lo = lax.bitcast_convert_type(u << 16, F32)
    hi = lax.bitcast_convert_type(u & jnp.uint32(0xFFFF0000), F32)
    return jnp.concatenate([lo, hi], axis=1).astype(BF16)


def _router_kernel(x_ref, g_ref, sc_ref, sh_ref, wh_ref, tri_ref, hp_ref, meta_ref, cnt_ref,
                   h_ref, run_ref):
    @pl.when(pl.program_id(0) == 0)
    def _():
        run_ref[...] = jnp.zeros_like(run_ref)

    g, sc, sh = g_ref[...], sc_ref[...], sh_ref[...]

    def norm_rows(r, carry):
        rows = pl.ds(pl.multiple_of(r * NORM_ROWS, NORM_ROWS), NORM_ROWS)
        hr = _norm_mod(x_ref[rows, :], g, sc, sh)
        h_ref[rows, :] = hr
        hp_ref[rows, :] = _pack_bf16_pairs(hr)
        return carry

    lax.fori_loop(0, x_ref.shape[0] // NORM_ROWS, norm_rows, 0, unroll=8)
    h = h_ref[...]
    h_hi = h.astype(BF16)
    h_lo = (h - h_hi.astype(F32)).astype(BF16)
    both = _dot(h_hi, wh_ref[...])
    logits = both[:, :LANES] + both[:, LANES:] + _dot(h_lo, wh_ref[:, :LANES])
    lane = lax.broadcasted_iota(I32, logits.shape, 1).astype(F32)
    l1 = jnp.where(lane < N_EXPERTS, logits, -jnp.inf)
    m1 = jnp.max(l1, axis=1, keepdims=True)
    i1 = jnp.min(jnp.where(l1 == m1, lane, float(LANES)), axis=1, keepdims=True)
    l2 = jnp.where(lane == i1, -jnp.inf, l1)
    m2 = jnp.max(l2, axis=1, keepdims=True)
    i2 = jnp.min(jnp.where(l2 == m2, lane, float(LANES)), axis=1, keepdims=True)
    e2 = jnp.exp(m2 - m1)
    w1 = 1.0 / (1.0 + e2)
    w2 = e2 / (1.0 + e2)
    member = jnp.where((lane == i1) | (lane == i2), 1.0, 0.0)
    run = run_ref[0:1, :]
    rank = _dot(tri_ref[...], member.astype(BF16)) + run
    p1 = jnp.sum(jnp.where(lane == i1, rank, 0.0), axis=1, keepdims=True)
    p2 = jnp.sum(jnp.where(lane == i2, rank, 0.0), axis=1, keepdims=True)
    run = run + jnp.sum(member, axis=0, keepdims=True)
    run_ref[0:1, :] = run
    cnt_ref[...] = jnp.broadcast_to(run, cnt_ref.shape)
    meta = jnp.zeros(logits.shape, F32)
    for idx, val in enumerate([i1, i2, w1, w2, p1, p2]):
        meta = jnp.where(lane == float(idx), val, meta)
    meta_ref[...] = meta


def _router(x2, g, mod4, wr_hi, wr_lo, seq):
    n, d = x2.shape
    tm = 512
    per_b = seq // tm
    tri = jnp.asarray(np.tril(np.ones((tm, tm), np.float32), -1), BF16)
    mod_spec = lambda k: pl.BlockSpec((None, None, 1, d), lambda i: (i // per_b, k, 0, 0))
    return pl.pallas_call(
        _router_kernel,
        grid=(n // tm,),
        in_specs=[
            pl.BlockSpec((tm, d), lambda i: (i, 0)),
            pl.BlockSpec((1, d), lambda i: (0, 0)),
            mod_spec(4), mod_spec(3),
            pl.BlockSpec((d, 2 * LANES), lambda i: (0, 0)),
            pl.BlockSpec((tm, tm), lambda i: (0, 0)),
        ],
        out_specs=[
            pl.BlockSpec((tm, d // 2), lambda i: (i, 0)),
            pl.BlockSpec((tm, LANES), lambda i: (i, 0)),
            pl.BlockSpec((8, LANES), lambda i: (0, 0)),
        ],
        out_shape=[jax.ShapeDtypeStruct((n, d // 2), jnp.uint32),
                   jax.ShapeDtypeStruct((n, LANES), F32),
                   jax.ShapeDtypeStruct((8, LANES), F32)],
        scratch_shapes=[pltpu.VMEM((tm, d), F32), pltpu.VMEM((8, LANES), F32)],
        compiler_params=_cparams(("arbitrary",)),
        name="router",
    )(x2, g, mod4, mod4, jnp.concatenate([wr_hi, wr_lo], axis=1), tri)


MOE_TM = 1024
DISP_TM = 512


def _dispatch_kernel(dest_hbm, hp_ref, xs_in_hbm, xs_hbm, idx_ref, sem_i, sem_r):
    del xs_in_hbm
    i = pl.program_id(0)
    n_idx = 2 * DISP_TM
    cp = pltpu.make_async_copy(dest_hbm.at[pl.ds(pl.multiple_of(i * n_idx, n_idx), n_idx)], idx_ref, sem_i)
    cp.start()
    cp.wait()

    def row_copy(r, k):
        return pltpu.make_async_copy(hp_ref.at[pl.ds(r, 1), :],
                                     xs_hbm.at[pl.ds(idx_ref[k * DISP_TM + r], 1), :], sem_r)

    def start(r, c):
        row_copy(r, 0).start()
        row_copy(r, 1).start()
        return c

    def wait(r, c):
        row_copy(r, 0).wait()
        row_copy(r, 1).wait()
        return c

    lax.fori_loop(0, DISP_TM, start, 0, unroll=8)
    lax.fori_loop(0, DISP_TM, wait, 0, unroll=8)


def _dispatch(dest_tiles, hp, n_rows):
    n, w = hp.shape
    xs0 = jnp.zeros((n_rows, w), jnp.uint32)
    return pl.pallas_call(
        _dispatch_kernel,
        grid=(n // DISP_TM,),
        in_specs=[
            pl.BlockSpec(memory_space=pl.ANY),
            pl.BlockSpec((DISP_TM, w), lambda i: (i, 0)),
            pl.BlockSpec(memory_space=pl.ANY),
        ],
        out_specs=pl.BlockSpec(memory_space=pl.ANY),
        out_shape=jax.ShapeDtypeStruct((n_rows, w), jnp.uint32),
        scratch_shapes=[
            pltpu.SMEM((2 * DISP_TM,), I32),
            pltpu.SemaphoreType.DMA(()),
            pltpu.SemaphoreType.DMA(()),
        ],
        input_output_aliases={2: 0},
        compiler_params=_cparams(("arbitrary",)),
        name="moe_dispatch",
    )(dest_tiles, hp, xs0)


MOE_SUB = 256


def _moe_kernel(te_ref, nu_ref, tr_ref, xs_ref, wg_ref, wu_ref, wd_ref, o_ref, xb_ref):
    t = pl.program_id(0)
    f = pl.program_id(1)
    rows = tr_ref[t]

    @pl.when(f == 0)
    def _():
        o_ref[...] = jnp.zeros_like(o_ref)

    @pl.when((rows > 0) & (f == 0))
    def _():
        xb_ref[...] = _unpack_bf16_pairs(xs_ref[...])

    @pl.when(rows == MOE_TM)
    def _():
        o_ref[...] += _swiglu_tile(xb_ref[...], wg_ref, wu_ref, wd_ref)

    for sb in range(MOE_TM // MOE_SUB):
        @pl.when((rows < MOE_TM) & (rows > sb * MOE_SUB))
        def _(sb=sb):
            sl = slice(sb * MOE_SUB, (sb + 1) * MOE_SUB)
            o_ref[sl, :] += _swiglu_tile(xb_ref[sl, :], wg_ref, wu_ref, wd_ref)


def _moe(tile_expert, n_used, tile_rows, xs, wg, wu, wd):
    d = wg.shape[1]
    n_tiles = tile_expert.shape[0]
    dff = wg.shape[2]
    tf = 512
    nf = dff // tf

    def f_blk(t, f, nu):
        return jnp.where(t < nu[0], f, nf - 1)

    grid_spec = pltpu.PrefetchScalarGridSpec(
        num_scalar_prefetch=3,
        grid=(n_tiles, nf),
        in_specs=[
            pl.BlockSpec((MOE_TM, d // 2), lambda t, f, te, nu, tr: (jnp.minimum(t, nu[0] - 1), 0)),
            pl.BlockSpec((None, d, tf), lambda t, f, te, nu, tr: (te[t], 0, f_blk(t, f, nu))),
            pl.BlockSpec((None, d, tf), lambda t, f, te, nu, tr: (te[t], 0, f_blk(t, f, nu))),
            pl.BlockSpec((None, tf, d), lambda t, f, te, nu, tr: (te[t], f_blk(t, f, nu), 0)),
        ],
        out_specs=pl.BlockSpec((MOE_TM, d), lambda t, f, te, nu, tr: (t, 0)),
        scratch_shapes=[pltpu.VMEM((MOE_TM, d), BF16)],
    )
    return pl.pallas_call(
        _moe_kernel,
        grid_spec=grid_spec,
        out_shape=jax.ShapeDtypeStruct((n_tiles * MOE_TM, d), F32),
        compiler_params=_cparams(("arbitrary", "arbitrary")),
        name="moe_experts",
    )(tile_expert, n_used, tile_rows, xs, wg, wu, wd)


COMB_TM = 512


def _combine_kernel(dest_hbm, y_hbm, x_ref, meta_ref, gt_ref, o_ref, idx_ref, y_ref, sem_i, sem_r):
    i = pl.program_id(0)
    n_idx = 2 * COMB_TM

    def row_copy(slot, j):
        return pltpu.make_async_copy(
            y_hbm.at[pl.ds(idx_ref[slot * n_idx + j], 1), :],
            y_ref.at[slot, pl.ds(j, 1), :], sem_r.at[slot])

    def fetch(step, slot):
        cp = pltpu.make_async_copy(
            dest_hbm.at[pl.ds(pl.multiple_of(step * n_idx, n_idx), n_idx)],
            idx_ref.at[pl.ds(pl.multiple_of(slot * n_idx, n_idx), n_idx)], sem_i)
        cp.start()
        cp.wait()

        def start(j, c):
            row_copy(slot, j).start()
            return c

        lax.fori_loop(0, n_idx, start, 0, unroll=8)

    @pl.when(i == 0)
    def _():
        fetch(i, 0)

    for slot in range(2):
        @pl.when(i % 2 == slot)
        def _(slot=slot):
            @pl.when(i + 1 < pl.num_programs(0))
            def _():
                fetch(i + 1, 1 - slot)

            def wait(j, c):
                row_copy(slot, j).wait()
                return c

            lax.fori_loop(0, n_idx, wait, 0, unroll=8)
            w1 = meta_ref[:, 2:3]
            w2 = meta_ref[:, 3:4]
            y1 = y_ref[slot, 0:COMB_TM, :]
            y2 = y_ref[slot, COMB_TM:n_idx, :]
            o_ref[...] = x_ref[...] + gt_ref[...] * (w1 * y1 + w2 * y2)


def _combine(dest, y_sorted, x2, meta, mod4, seq):
    n, d = x2.shape
    tm = COMB_TM
    per_b = seq // tm
    return pl.pallas_call(
        _combine_kernel,
        grid=(n // tm,),
        in_specs=[
            pl.BlockSpec(memory_space=pl.ANY),
            pl.BlockSpec(memory_space=pl.ANY),
            pl.BlockSpec((tm, d), lambda i: (i, 0)),
            pl.BlockSpec((tm, LANES), lambda i: (i, 0)),
            pl.BlockSpec((None, None, 1, d), lambda i: (i // per_b, 5, 0, 0)),
        ],
        out_specs=pl.BlockSpec((tm, d), lambda i: (i, 0)),
        out_shape=jax.ShapeDtypeStruct((n, d), F32),
        scratch_shapes=[
            pltpu.SMEM((2 * 2 * tm,), I32),
            pltpu.VMEM((2, 2 * tm, d), F32),
            pltpu.SemaphoreType.DMA(()),
            pltpu.SemaphoreType.DMA((2,)),
        ],
        compiler_params=_cparams(("arbitrary",)),
        name="moe_combine",
    )(dest, y_sorted, x2, meta, mod4)


def _moe_layer(x2, g, mod4, w_router, wg, wu, wd, seq):
    n, d = x2.shape
    wr = jnp.zeros((d, LANES), F32).at[:, :N_EXPERTS].set(w_router)
    wr_hi = wr.astype(BF16)
    wr_lo = (wr - wr_hi.astype(F32)).astype(BF16)
    hp, meta, cnt = _router(x2, g, mod4, wr_hi, wr_lo, seq)

    n_tiles = 2 * n // MOE_TM + N_EXPERTS
    counts = cnt[0, :N_EXPERTS].astype(I32)
    tiles_per = (counts + MOE_TM - 1) // MOE_TM
    tile_end = jnp.cumsum(tiles_per)
    row_start = (tile_end - tiles_per) * MOE_TM
    n_used = tile_end[-1]
    idx = meta[:, 0:2].astype(I32)
    pos = meta[:, 4:6].astype(I32)
    start_of = sum(jnp.where(idx == e, row_start[e], 0) for e in range(N_EXPERTS))
    dest = start_of + pos
    all_tiles = jnp.arange(n_tiles, dtype=I32)
    tile_ids = jnp.minimum(all_tiles, n_used - 1)
    tile_expert = jnp.sum(tile_ids[:, None] >= tile_end[None, :], axis=1).astype(I32)
    onehot_e = tile_expert[:, None] == jnp.arange(N_EXPERTS, dtype=I32)[None, :]
    first_tile = jnp.sum(jnp.where(onehot_e, (tile_end - tiles_per)[None, :], 0), axis=1)
    cnt_tile = jnp.sum(jnp.where(onehot_e, counts[None, :], 0), axis=1)
    tile_rows = jnp.clip(cnt_tile - (all_tiles - first_tile) * MOE_TM, 0, MOE_TM)
    tile_rows = jnp.where(all_tiles < n_used, tile_rows, 0).astype(I32)
    assert COMB_TM == DISP_TM
    dest_tiles = dest.reshape(n // COMB_TM, COMB_TM, 2).transpose(0, 2, 1).reshape(-1)

    xs = _dispatch(dest_tiles, hp, n_tiles * MOE_TM)
    y_sorted = _moe(tile_expert, n_used.reshape(1).astype(I32), tile_rows, xs, wg, wu, wd)
    return _combine(dest_tiles, y_sorted, x2, meta, mod4, seq)


def _t5_bucket(dist):
    max_exact = N_BUCKETS // 2
    d = np.maximum(dist, 1).astype(np.float32)
    large = max_exact + (np.log(d / max_exact) / np.log(MAX_DIST / max_exact)
                         * (N_BUCKETS - max_exact)).astype(np.int32)
    large = np.minimum(large, N_BUCKETS - 1)
    return np.where(dist < max_exact, dist, large).astype(np.int32)


def _pack_w_in(w):
    splits = [M_HEADS * M_QK, M_HEADS * M_QK, M_HEADS * M_V, M_HEADS * M_V, 2 * M_HEADS,
              G_HEADS * G_QK, G_HEADS * G_QK, G_HEADS * G_V, G_HEADS * G_V, G_RANK,
              A_HEADS * A_HD, A_KV * A_HD, A_KV * A_HD]
    offs = np.concatenate([[0], np.cumsum(splits)])
    seg = lambda i: w[:, offs[i]:offs[i + 1]]
    main = jnp.concatenate([seg(i) for i in (0, 1, 2, 3, 5, 6, 7, 8, 10, 11, 12)], axis=1)
    small = jnp.concatenate(
        [seg(4), seg(9), jnp.zeros((w.shape[0], LANES - 2 * M_HEADS - G_RANK), w.dtype)], axis=1)
    return main.astype(BF16), small.astype(BF16)


def kernel(x, c, w_ada, b_ada, g_mix_norm, g_ffn_norm, w_in, b_gates_m, w_conv_m, b_conv_m, g_out_m,
           w_gla_decay, b_gla_decay, g_out_g, g_qnorm, g_knorm, sinks, rel_bias, w_out, w_ffn_gate,
           w_ffn_up, w_ffn_down, w_router, w_moe_gate, w_moe_up, w_moe_down):
    bsz, seq, d = x.shape
    depth = w_ada.shape[0]
    n = bsz * seq
    assert seq % 1024 == 0 and d == 2048 and n % MOE_TM == 0 and bsz % max(MLSTM_NB, GLA_NB, SWA_NB) == 0

    mod = _ada(c, w_ada, b_ada)

    jj = np.arange(BLK)[:, None]
    ss = np.arange(2 * BLK)[None, :]
    buckets = _t5_bucket(np.clip(jj + BLK - ss, 0, None))
    onehot = jnp.asarray(np.eye(N_BUCKETS, dtype=np.float32)[buckets.reshape(-1)])
    bias = jnp.einsum("pb,bh->hp", onehot, rel_bias.astype(F32),
                      precision=lax.Precision.HIGHEST).reshape(A_HEADS, BLK, 2 * BLK)
    in_window = (jj + BLK - ss >= 0) & (jj + BLK - ss < WINDOW)
    masks = np.stack([in_window & (ss >= BLK), in_window])
    bias = jnp.where(jnp.asarray(masks)[:, None], (bias * LOG2E)[None], -jnp.inf)
    ind = jnp.asarray(np.kron(np.eye(LANES // A_HD), np.ones((A_HD, A_HD))) / A_HD, BF16)

    x2 = x.reshape(n, d)
    for l in range(depth):
        mod4 = mod[l].reshape(bsz, 6, 1, d)
        w_main, w_small = _pack_w_in(w_in[l])
        pm, ps = _inproj(x2, g_mix_norm[l].reshape(1, d), mod4, w_main, w_small, seq)
        pm = pm.reshape(bsz, seq, -1)
        ps = ps.reshape(bsz, seq, -1)

        gb = jnp.zeros((1, LANES), F32).at[0, :2 * M_HEADS].set(b_gates_m[l].reshape(-1))
        ym = _mlstm(pm, ps, w_conv_m[l], b_conv_m[l].reshape(1, -1), gb, g_out_m[l].reshape(1, -1))
        wdec = jnp.zeros((LANES, G_HEADS * G_QK), F32).at[S_GA:S_GA + G_RANK].set(w_gla_decay[l])
        yg = _gla(pm, ps, wdec, b_gla_decay[l].reshape(1, -1), g_out_g[l].reshape(1, -1))
        gq = jnp.tile(g_qnorm[l], A_HEADS).reshape(1, -1) * (A_HD ** -0.5 * LOG2E)
        gk = jnp.tile(g_knorm[l], A_KV).reshape(1, -1)
        ya = _swa(pm, bias, gq, gk, sinks[l] * LOG2E, ind)
        x2 = _outproj(ym.reshape(n, -1), yg.reshape(n, -1), ya.reshape(n, -1), w_out[l].astype(BF16),
                      x2, mod4, seq)

        gf = g_ffn_norm[l].reshape(1, d)
        if l % 2 == 0:
            j = l // 2
            x2 = _ffn(x2, gf, mod4, w_ffn_gate[j], w_ffn_up[j], w_ffn_down[j], seq)
        else:
            j = l // 2
            x2 = _moe_layer(x2, gf, mod4, w_router[j], w_moe_gate[j], w_moe_up[j], w_moe_down[j], seq)
    return x2.reshape(bsz, seq, d)
```

```python
import functools

import numpy as np
import jax
import jax.numpy as jnp
from jax import lax
from jax.experimental import pallas as pl
from jax.experimental.pallas import tpu as pltpu

F32 = jnp.float32
BF16 = jnp.bfloat16
I32 = jnp.int32

M_HEADS = 4
M_QK = 64
M_V = 128
CONV_W = 4
G_HEADS = 4
G_QK = 64
G_V = 128
G_RANK = 16
G_TAU = 16.0
G_CHUNK = 64
G_SUB = 16
A_HD = 64
A_HEADS = 16
A_KV = 2
WINDOW = 128
N_BUCKETS = 32
MAX_DIST = 128
N_EXPERTS = 8
EPS = 1e-6
LOG2E = 1.4426950408889634

LANES = 128
BLK = 128
VMEM_LIMIT = 60000 * 1024

C_MQK, C_MV, C_MO = 0, 512, 1024
C_GQK, C_GV, C_GG = 1536, 2048, 2560
C_AQ, C_AK, C_AV = 3072, 4096, 4224
N_MAIN = 4352
S_GA = 8


def _cparams(sem):
    return pltpu.CompilerParams(dimension_semantics=sem, vmem_limit_bytes=VMEM_LIMIT)


def _dot(a, b):
    return jnp.dot(a, b, preferred_element_type=F32)


def _dot_nt(a, b):
    return lax.dot_general(a, b, (((1,), (1,)), ((), ())), preferred_element_type=F32)


def _split_dot(a, b_f32):
    hi = b_f32.astype(BF16)
    lo = (b_f32 - hi.astype(F32)).astype(BF16)
    return _dot(a, hi) + _dot(a, lo)


def _sigmoid(x):
    return 1.0 / (1.0 + jnp.exp(-x))


def _log_sigmoid(x):
    return jnp.minimum(x, 0.0) - jnp.log(1.0 + jnp.exp(-jnp.abs(x)))


def _norm_mod(x, g, sc, sh):
    ms = jnp.mean(x * x, axis=-1, keepdims=True)
    return (x * lax.rsqrt(ms + EPS) * g) * (1.0 + sc) + sh


NORM_ROWS = 16


def _norm_mod_rows(x_ref, g_ref, sc_ref, sh_ref, out_refs):
    g, sc, sh = g_ref[...], sc_ref[...], sh_ref[...]

    def body(r, carry):
        rows = pl.ds(pl.multiple_of(r * NORM_ROWS, NORM_ROWS), NORM_ROWS)
        h = _norm_mod(x_ref[rows, :], g, sc, sh)
        for ref in out_refs:
            ref[rows, :] = h.astype(ref.dtype)
        return carry

    lax.fori_loop(0, x_ref.shape[0] // NORM_ROWS, body, 0, unroll=8)


def _ada_kernel(c_ref, w_ref, b_ref, o_ref):
    c = c_ref[...]
    cond = (c * _sigmoid(c)).astype(BF16)
    o_ref[...] = _dot(cond, w_ref[...].astype(BF16)) + b_ref[...]


def _ada(c, w_ada, b_ada):
    depth, d, n6 = w_ada.shape
    bsz = c.shape[0]
    tn = 1024
    return pl.pallas_call(
        _ada_kernel,
        grid=(depth, n6 // tn),
        in_specs=[
            pl.BlockSpec((bsz, d), lambda l, n: (0, 0)),
            pl.BlockSpec((None, d, tn), lambda l, n: (l, 0, n)),
            pl.BlockSpec((None, 1, tn), lambda l, n: (l, 0, n)),
        ],
        out_specs=pl.BlockSpec((None, bsz, tn), lambda l, n: (l, 0, n)),
        out_shape=jax.ShapeDtypeStruct((depth, bsz, n6), F32),
        compiler_params=_cparams(("arbitrary", "arbitrary")),
        name="ada_mod",
    )(c, w_ada, b_ada.reshape(depth, 1, n6))


def _inproj_kernel(x_ref, g_ref, sc_ref, sh_ref, wm_ref, ws_ref, om_ref, os_ref, h_ref):
    @pl.when(pl.program_id(1) == 0)
    def _():
        _norm_mod_rows(x_ref, g_ref, sc_ref, sh_ref, [h_ref])
        os_ref[...] = _dot(h_ref[...], ws_ref[...])

    om_ref[...] = _dot(h_ref[...], wm_ref[...]).astype(BF16)


def _inproj(x2, g, mod4, w_main, w_small, seq):
    n, d = x2.shape
    tm, tn = 1024, N_MAIN // 2
    per_b = seq // tm
    mod_spec = lambda k: pl.BlockSpec((None, None, 1, d), lambda i, j: (i // per_b, k, 0, 0))
    return pl.pallas_call(
        _inproj_kernel,
        grid=(n // tm, N_MAIN // tn),
        in_specs=[
            pl.BlockSpec((tm, d), lambda i, j: (i, 0)),
            pl.BlockSpec((1, d), lambda i, j: (0, 0)),
            mod_spec(1), mod_spec(0),
            pl.BlockSpec((d, tn), lambda i, j: (0, j)),
            pl.BlockSpec((d, LANES), lambda i, j: (0, 0)),
        ],
        out_specs=[
            pl.BlockSpec((tm, tn), lambda i, j: (i, j)),
            pl.BlockSpec((tm, LANES), lambda i, j: (i, 0)),
        ],
        out_shape=[jax.ShapeDtypeStruct((n, N_MAIN), BF16),
                   jax.ShapeDtypeStruct((n, LANES), F32)],
        scratch_shapes=[pltpu.VMEM((tm, d), BF16)],
        compiler_params=_cparams(("arbitrary", "arbitrary")),
        name="in_proj",
    )(x2, g, mod4, mod4, w_main, w_small)


def _per_batch(body, n_batch_refs, n_state_refs):
    def kern(*refs):
        n_shared = len(refs) - n_batch_refs - n_state_refs - 1
        batch_in, shared = refs[:n_batch_refs], refs[n_batch_refs:n_batch_refs + n_shared]
        y_ref = refs[n_batch_refs + n_shared]
        state = refs[len(refs) - n_state_refs:]

        @pl.when(pl.program_id(1) == 0)
        def _():
            for st in state:
                st[...] = jnp.zeros_like(st)

        for bb in range(y_ref.shape[0]):
            body(*[r.at[bb] for r in batch_in], *shared, y_ref.at[bb], *[st.at[bb] for st in state])
    return kern


def _mlstm_body(qk_ref, v_ref, o_ref, sm_ref, wc_ref, bc_ref, gb_ref, go_ref, y_ref,
                tail_ref, ct_ref, n_ref, m_ref):
    L = BLK
    nqk = M_HEADS * M_QK

    x = qk_ref[...].astype(F32)
    prev = tail_ref[...]
    row8 = lax.broadcasted_iota(I32, prev.shape, 0)
    acc = x * wc_ref[CONV_W - 1:CONV_W, :] + bc_ref[...]
    for dlt in range(1, CONV_W):
        xs = pltpu.roll(x, dlt, axis=0)
        ps = pltpu.roll(prev, dlt, axis=0)
        top = jnp.where(row8 < dlt, ps, xs[0:8])
        xs = jnp.concatenate([top, xs[8:]], axis=0)
        acc = acc + xs * wc_ref[CONV_W - 1 - dlt:CONV_W - dlt, :]
    tail_ref[...] = x[L - 8:L]
    qk = acc * _sigmoid(acc)
    q = qk[:, :nqk]
    k = qk[:, nqk:] * (M_QK ** -0.5)
    kt_b = k.T.astype(BF16)

    gates = sm_ref[...] + gb_ref[...]
    row = lax.broadcasted_iota(I32, (L, L), 0)
    col = lax.broadcasted_iota(I32, (L, L), 1)
    causal = row >= col
    tri = causal.astype(BF16)
    bcum = _split_dot(tri, _log_sigmoid(gates))
    z = jnp.where(col < M_HEADS, gates, bcum)
    zt = z.T
    lane_q = lax.broadcasted_iota(I32, (L, nqk), 1)

    n_all = n_ref[...]
    head_n = lax.broadcasted_iota(I32, n_all.shape, 1) // M_QK
    row_n = lax.broadcasted_iota(I32, n_all.shape, 0)
    qn = _dot_nt(q.astype(BF16), jnp.where(head_n == row_n, n_all, 0.0).astype(BF16))
    lane_w = lax.broadcasted_iota(I32, (L, LANES), 1)
    w_cols = jnp.zeros((L, LANES), F32)
    decay_rows = jnp.zeros((n_all.shape[0], 1), F32)

    outs = []
    for h in range(M_HEADS):
        in_head = (lane_q >= h * M_QK) & (lane_q < (h + 1) * M_QK)
        qm = jnp.where(in_head, q, 0.0)
        qm_b = qm.astype(BF16)
        v_h = v_ref[:, h * M_V:(h + 1) * M_V]
        m_prev = m_ref[h:h + 1, 0:1]
        ct_h = ct_ref[h]
        bcol = z[:, M_HEADS + h:M_HEADS + h + 1]
        icol = z[:, h:h + 1]
        brow = zt[M_HEADS + h:M_HEADS + h + 1, :]
        irow = zt[h:h + 1, :]
        logd = jnp.where(causal, bcol - brow + irow, -jnp.inf)
        log_inter = bcol + m_prev
        m_row = jnp.maximum(log_inter, jnp.max(logd, axis=1, keepdims=True))
        dmat = jnp.exp(logd - m_row)
        w_inter = jnp.exp(log_inter - m_row)
        s = _dot(qm_b, kt_b) * dmat
        num = w_inter * _dot(qm_b, ct_h.astype(BF16)) + _dot(s.astype(BF16), v_h)
        den = w_inter * qn[:, h:h + 1] + jnp.sum(s, axis=1, keepdims=True)
        outs.append(num / jnp.maximum(jnp.abs(den), jnp.exp(-m_row)))
        b_last = bcol[L - 1:L, :]
        log_w = b_last - bcol + icol
        m_new = jnp.maximum(b_last + m_prev, jnp.max(log_w, axis=0, keepdims=True))
        w = jnp.exp(log_w - m_new)
        decay = jnp.exp(b_last + m_prev - m_new)
        vw = (v_h.astype(F32) * w).astype(BF16)
        ct_ref[h] = decay * ct_h + _dot(kt_b, vw)
        w_cols = jnp.where(lane_w == h, w, w_cols)
        decay_rows = jnp.where(row_n[:, 0:1] == h, decay, decay_rows)
        m_ref[h:h + 1, :] = jnp.broadcast_to(m_new, (1, LANES))

    n_upd = _dot(w_cols.T[0:n_all.shape[0]].astype(BF16), k.astype(BF16))
    n_ref[...] = decay_rows * n_all + n_upd

    og = _sigmoid(o_ref[...].astype(F32))
    for h in range(M_HEADS):
        hh = outs[h]
        ms = jnp.mean(hh * hh, axis=1, keepdims=True)
        sl = slice(h * M_V, (h + 1) * M_V)
        y = hh * lax.rsqrt(ms + EPS) * go_ref[:, sl]
        y_ref[:, sl] = (y * og[:, sl]).astype(BF16)


MLSTM_NB, GLA_NB, SWA_NB = 1, 2, 2


def _mixer_specs(bsz, seq, nb):
    rows = lambda width, cb: pl.BlockSpec((nb, BLK, width), lambda b, c: (b, c, cb))
    full = lambda shp: pl.BlockSpec(shp, lambda b, c: (0,) * len(shp))
    return rows, full, (bsz // nb, seq // BLK)


def _mlstm(pm, ps, wc, bc, gb, go):
    bsz, seq, _ = pm.shape
    wd = M_HEADS * M_V
    MIX_NB = MLSTM_NB
    rows, full, grid = _mixer_specs(bsz, seq, MIX_NB)
    return pl.pallas_call(
        _per_batch(_mlstm_body, 4, 4),
        grid=grid,
        in_specs=[
            rows(wd, C_MQK // wd), rows(wd, C_MV // wd), rows(wd, C_MO // wd), rows(LANES, 0),
            full((CONV_W, 2 * M_HEADS * M_QK)), full((1, 2 * M_HEADS * M_QK)),
            full((1, LANES)), full((1, wd)),
        ],
        out_specs=rows(wd, 0),
        out_shape=jax.ShapeDtypeStruct((bsz, seq, wd), BF16),
        scratch_shapes=[
            pltpu.VMEM((MIX_NB, 8, 2 * M_HEADS * M_QK), F32),
            pltpu.VMEM((MIX_NB, M_HEADS, M_HEADS * M_QK, M_V), F32),
            pltpu.VMEM((MIX_NB, 8, M_HEADS * M_QK), F32),
            pltpu.VMEM((MIX_NB, 8, LANES), F32),
        ],
        compiler_params=_cparams(("arbitrary", "arbitrary")),
        name="mlstm",
    )(pm, pm, pm, ps, wc, bc, gb, go)


def _gla_body(qk_ref, v_ref, g_ref, sm_ref, wd_ref, bd_ref, go_ref, y_ref, s_ref):
    L, C, SC = BLK, G_CHUNK, G_SUB
    nqk = G_HEADS * G_QK
    pair_k = 2 * G_QK
    pair_v = 2 * G_V
    n_pairs = G_HEADS // 2

    zdec = _dot(sm_ref[...].astype(BF16), wd_ref[...].astype(BF16)) + bd_ref[...]
    log_a = _log_sigmoid(zdec) * (LOG2E / G_TAU)
    row = lax.broadcasted_iota(I32, (L, L), 0)
    col = lax.broadcasted_iota(I32, (L, L), 1)
    same_chunk = (row // C) == (col // C)
    tri = ((row >= col) & same_chunk).astype(BF16)
    b = _split_dot(tri, log_a)
    bt = b.T

    q = qk_ref[:, :nqk].astype(F32) * (G_QK ** -0.5)
    k = qk_ref[:, nqk:].astype(F32)
    b_end = jnp.concatenate(
        [jnp.broadcast_to(b[(cc + 1) * C - 1:(cc + 1) * C, :], (C, nqk)) for cc in range(L // C)], axis=0)
    kdt = (k * jnp.exp2(b_end - b)).T
    qe = q * jnp.exp2(b)

    dloc = col - (row // SC) * SC
    diag_sel = jnp.where((dloc >= 0) & (dloc <= row % SC), dloc, -1)
    sub_r = (row % C) // SC
    off_sel = jnp.where(same_chunk & ((col % C) // SC < sub_r), sub_r, -1)
    diag_sel2 = jnp.concatenate([diag_sel, diag_sel], axis=1)
    off_sel2 = jnp.concatenate([off_sel, off_sel], axis=1)

    lane_s = lax.broadcasted_iota(I32, (pair_k, L), 1)
    head_k = lax.broadcasted_iota(I32, (L, pair_k), 1) // G_QK
    rblk = lax.broadcasted_iota(I32, (pair_k, pair_v), 0) // G_QK
    cblk = lax.broadcasted_iota(I32, (pair_k, pair_v), 1) // G_V
    blockdiag = rblk == cblk
    vrow = lax.broadcasted_iota(I32, (2 * L, pair_v), 0) // L
    vcol = lax.broadcasted_iota(I32, (2 * L, pair_v), 1) // G_V
    vdiag = vrow == vcol

    def stack_heads(x):
        return jnp.concatenate([jnp.where(head_k == hh, x, 0.0) for hh in range(2)], axis=0).astype(BF16)

    pair_out = []
    for p in range(n_pairs):
        kl = slice(p * pair_k, (p + 1) * pair_k)
        vl = slice(p * pair_v, (p + 1) * pair_v)
        bp, qp, kp = b[:, kl], q[:, kl], k[:, kl]
        v_p = v_ref[:, vl]
        b3 = bp.reshape(L // SC, SC, pair_k)

        def sub_row(i, b3=b3):
            return jnp.broadcast_to(b3[:, i:i + 1, :], b3.shape).reshape(L, pair_k)

        kst = stack_heads(kp)
        att = jnp.zeros((L, 2 * L), F32)
        for i in range(SC):
            e = jnp.exp2(jnp.minimum(bp - sub_row(i), 0.0))
            r = _dot_nt((qp * e).astype(BF16), kst)
            att = jnp.where(diag_sel2 == i, r, att)

        qs = (qp * jnp.exp2(bp - sub_row(0))).astype(BF16)
        for sub in range(1, C // SC):
            bref = jnp.concatenate(
                [jnp.broadcast_to(bp[cc * C + sub * SC:cc * C + sub * SC + 1, :], (C, pair_k))
                 for cc in range(L // C)], axis=0)
            ks = kp * jnp.exp2(jnp.minimum(bref - bp, 0.0))
            att = jnp.where(off_sel2 == sub, _dot_nt(qs, stack_heads(ks)), att)

        vblk = jnp.where(vdiag, jnp.concatenate([v_p, v_p], axis=0), 0)
        intra = _dot(att.astype(BF16), vblk)

        outs = []
        for cc in range(L // C):
            r0 = cc * C
            s_p = s_ref[p]
            outs.append(_dot(qe[r0:r0 + C, kl].astype(BF16), s_p.astype(BF16)) + intra[r0:r0 + C])
            in_chunk = (lane_s >= r0) & (lane_s < r0 + C)
            kd = jnp.where(in_chunk, kdt[kl, :], 0.0).astype(BF16)
            upd = jnp.where(blockdiag, _dot(kd, v_p), 0.0)
            dcol = jnp.exp2(bt[kl, r0 + C - 1:r0 + C])
            s_ref[p] = dcol * s_p + upd
        pair_out.append(outs)

    gate = g_ref[...].astype(F32)
    gate = gate * _sigmoid(gate)
    for p in range(n_pairs):
        o_p = jnp.concatenate(pair_out[p], axis=0)
        for hh in range(2):
            h = 2 * p + hh
            oh = o_p[:, hh * G_V:(hh + 1) * G_V]
            ms = jnp.mean(oh * oh, axis=1, keepdims=True)
            sl = slice(h * G_V, (h + 1) * G_V)
            y = oh * lax.rsqrt(ms + EPS) * go_ref[:, sl]
            y_ref[:, sl] = (y * gate[:, sl]).astype(BF16)


def _gla(pm, ps, wd, bd, go):
    bsz, seq, _ = pm.shape
    w = G_HEADS * G_V
    nqk = G_HEADS * G_QK
    MIX_NB = GLA_NB
    rows, full, grid = _mixer_specs(bsz, seq, MIX_NB)
    return pl.pallas_call(
        _per_batch(_gla_body, 4, 1),
        grid=grid,
        in_specs=[
            rows(w, C_GQK // w), rows(w, C_GV // w), rows(w, C_GG // w), rows(LANES, 0),
            full((LANES, nqk)), full((1, nqk)), full((1, w)),
        ],
        out_specs=rows(w, 0),
        out_shape=jax.ShapeDtypeStruct((bsz, seq, w), BF16),
        scratch_shapes=[
            pltpu.VMEM((MIX_NB, G_HEADS // 2, 2 * G_QK, 2 * G_V), F32),
        ],
        compiler_params=_cparams(("arbitrary", "arbitrary")),
        name="gla",
    )(pm, pm, pm, ps, wd, bd, go)


def _group_mean_sq(x, ind):
    return _split_dot_rhs(x * x, ind)


def _split_dot_rhs(a_f32, b_bf16):
    hi = a_f32.astype(BF16)
    lo = (a_f32 - hi.astype(F32)).astype(BF16)
    return _dot(hi, b_bf16) + _dot(lo, b_bf16)


def _swa_body(q_ref, kc_ref, kp_ref, vc_ref, vp_ref, bias_ref, gq_ref, gk_ref, sink_ref, ind_ref,
              y_ref):
    T = BLK
    heads_per_tile = LANES // A_HD
    n_tiles = A_HEADS // heads_per_tile
    grp = A_HEADS // A_KV
    ind = ind_ref[...]

    qn = []
    for t in range(n_tiles):
        sl = slice(t * LANES, (t + 1) * LANES)
        qt = q_ref[:, sl].astype(F32)
        ms = _group_mean_sq(qt, ind)
        qn.append((qt * lax.rsqrt(ms + EPS) * gq_ref[:, sl]).astype(BF16))
    kk = jnp.concatenate([kp_ref[...], kc_ref[...]], axis=0).astype(F32)
    kn = kk * lax.rsqrt(_group_mean_sq(kk, ind) + EPS) * gk_ref[...]
    vv = jnp.concatenate([vp_ref[...], vc_ref[...]], axis=0).astype(F32)
    k_sw = pltpu.roll(kn, A_HD, axis=1)
    v_sw = pltpu.roll(vv, A_HD, axis=1)
    half = lax.broadcasted_iota(I32, (2 * T, LANES), 1) // A_HD

    acc = [None] * n_tiles
    for g in range(A_KV):
        for p in range(heads_per_tile):
            ksrc, vsrc = (kn, vv) if g == p else (k_sw, v_sw)
            kz = jnp.where(half == p, ksrc, 0.0).astype(BF16)
            vz = jnp.where(half == p, vsrc, 0.0).astype(BF16)
            for u in range(grp // heads_per_tile):
                h = g * grp + heads_per_tile * u + p
                t = h // heads_per_tile
                sc = _dot_nt(qn[t], kz) + bias_ref[h]
                sink = sink_ref[h]
                m = jnp.maximum(jnp.max(sc, axis=1, keepdims=True), sink)
                pr = jnp.exp2(sc - m)
                den = jnp.sum(pr, axis=1, keepdims=True) + jnp.exp2(sink - m)
                o = _dot(pr.astype(BF16), vz) / den
                acc[t] = o if acc[t] is None else acc[t] + o
    for t in range(n_tiles):
        y_ref[:, t * LANES:(t + 1) * LANES] = acc[t].astype(BF16)


def _swa(pm, bias, gq, gk, sinks, ind):
    bsz, seq, _ = pm.shape
    wq = A_HEADS * A_HD
    MIX_NB = SWA_NB
    rows, full, grid = _mixer_specs(bsz, seq, MIX_NB)
    cur = lambda cb: rows(LANES, cb)
    prv = lambda cb: pl.BlockSpec((MIX_NB, BLK, LANES), lambda b, c: (b, jnp.maximum(c - 1, 0), cb))
    return pl.pallas_call(
        _per_batch(_swa_body, 5, 0),
        grid=grid,
        in_specs=[
            rows(wq, C_AQ // wq),
            cur(C_AK // LANES), prv(C_AK // LANES), cur(C_AV // LANES), prv(C_AV // LANES),
            pl.BlockSpec((None, A_HEADS, BLK, 2 * BLK), lambda b, c: (jnp.minimum(c, 1), 0, 0, 0)),
            full((1, wq)), full((1, LANES)),
            pl.BlockSpec(memory_space=pltpu.SMEM),
            full((LANES, LANES)),
        ],
        out_specs=rows(wq, 0),
        out_shape=jax.ShapeDtypeStruct((bsz, seq, wq), BF16),
        compiler_params=_cparams(("arbitrary", "arbitrary")),
        name="swa",
    )(pm, pm, pm, pm, pm, bias, gq, gk, sinks, ind)


def _outproj_kernel(ym_ref, yg_ref, ya_ref, w_ref, x_ref, gt_ref, o_ref):
    y = jnp.concatenate([ym_ref[...], yg_ref[...], ya_ref[...]], axis=1)
    o_ref[...] = x_ref[...] + gt_ref[...] * _dot(y, w_ref[...])


def _outproj(ym, yg, ya, w, x2, mod4, seq):
    n, d = x2.shape
    tm, tn = 1024, d
    per_b = seq // tm
    return pl.pallas_call(
        _outproj_kernel,
        grid=(n // tm, d // tn),
        in_specs=[
            pl.BlockSpec((tm, ym.shape[1]), lambda i, j: (i, 0)),
            pl.BlockSpec((tm, yg.shape[1]), lambda i, j: (i, 0)),
            pl.BlockSpec((tm, ya.shape[1]), lambda i, j: (i, 0)),
            pl.BlockSpec((d, tn), lambda i, j: (0, j), pipeline_mode=pl.Buffered(1)),
            pl.BlockSpec((tm, tn), lambda i, j: (i, j)),
            pl.BlockSpec((None, None, 1, tn), lambda i, j: (i // per_b, 2, 0, j)),
        ],
        out_specs=pl.BlockSpec((tm, tn), lambda i, j: (i, j)),
        out_shape=jax.ShapeDtypeStruct((n, d), F32),
        compiler_params=_cparams(("arbitrary", "arbitrary")),
        name="out_proj",
    )(ym, yg, ya, w, x2, mod4)


def _swiglu_tile(h, wg_ref, wu_ref, wd_ref):
    gate = _dot(h, wg_ref[...].astype(BF16))
    up = _dot(h, wu_ref[...].astype(BF16))
    a = (gate * _sigmoid(gate) * up).astype(BF16)
    return _dot(a, wd_ref[...].astype(BF16))


def _ffn_kernel(x_ref, g_ref, sc_ref, sh_ref, gt_ref, wg_ref, wu_ref, wd_ref, o_ref, h_ref):
    f = pl.program_id(1)

    @pl.when(f == 0)
    def _():
        _norm_mod_rows(x_ref, g_ref, sc_ref, sh_ref, [h_ref])
        o_ref[...] = jnp.zeros_like(o_ref)

    o_ref[...] += _swiglu_tile(h_ref[...], wg_ref, wu_ref, wd_ref)

    @pl.when(f == pl.num_programs(1) - 1)
    def _():
        o_ref[...] = x_ref[...] + gt_ref[...] * o_ref[...]


def _ffn(x2, g, mod4, wg, wu, wd, seq):
    n, d = x2.shape
    dff = wg.shape[1]
    tm, tf = 1024, 512
    per_b = seq // tm
    mod_spec = lambda k: pl.BlockSpec((None, None, 1, d), lambda i, f: (i // per_b, k, 0, 0))
    return pl.pallas_call(
        _ffn_kernel,
        grid=(n // tm, dff // tf),
        in_specs=[
            pl.BlockSpec((tm, d), lambda i, f: (i, 0), pipeline_mode=pl.Buffered(1)),
            pl.BlockSpec((1, d), lambda i, f: (0, 0)),
            mod_spec(4), mod_spec(3), mod_spec(5),
            pl.BlockSpec((d, tf), lambda i, f: (0, f)),
            pl.BlockSpec((d, tf), lambda i, f: (0, f)),
            pl.BlockSpec((tf, d), lambda i, f: (f, 0)),
        ],
        out_specs=pl.BlockSpec((tm, d), lambda i, f: (i, 0)),
        out_shape=jax.ShapeDtypeStruct((n, d), F32),
        scratch_shapes=[pltpu.VMEM((tm, d), BF16)],
        compiler_params=_cparams(("arbitrary", "arbitrary")),
        name="ffn_dense",
    )(x2, g, mod4, mod4, mod4, wg, wu, wd)


def _pack_bf16_pairs(h):
    w = h.shape[1] // 2
    bits = lax.bitcast_convert_type(h.astype(BF16).astype(F32), jnp.uint32)
    return (bits[:, :w] >> 16) | (bits[:, w:] & jnp.uint32(0xFFFF0000))


def _unpack_bf16_pairs(u):
    lo = lax.bitcast_convert_type(u << 16, F32)
    hi = lax.bitcast_convert_type(u & jnp.uint32(0xFFFF0000), F32)
    return jnp.concatenate([lo, hi], axis=1).astype(BF16)


def _router_kernel(x_ref, g_ref, sc_ref, sh_ref, wh_ref, tri_ref, hp_ref, meta_ref, cnt_ref,
                   h_ref, run_ref):
    @pl.when(pl.program_id(0) == 0)
    def _():
        run_ref[...] = jnp.zeros_like(run_ref)

    g, sc, sh = g_ref[...], sc_ref[...], sh_ref[...]

    def norm_rows(r, carry):
        rows = pl.ds(pl.multiple_of(r * NORM_ROWS, NORM_ROWS), NORM_ROWS)
        hr = _norm_mod(x_ref[rows, :], g, sc, sh)
        h_ref[rows, :] = hr
        hp_ref[rows, :] = _pack_bf16_pairs(hr)
        return carry

    lax.fori_loop(0, x_ref.shape[0] // NORM_ROWS, norm_rows, 0, unroll=8)
    h = h_ref[...]
    h_hi = h.astype(BF16)
    h_lo = (h - h_hi.astype(F32)).astype(BF16)
    both = _dot(h_hi, wh_ref[...])
    logits = both[:, :LANES] + both[:, LANES:] + _dot(h_lo, wh_ref[:, :LANES])
    lane = lax.broadcasted_iota(I32, logits.shape, 1).astype(F32)
    l1 = jnp.where(lane < N_EXPERTS, logits, -jnp.inf)
    m1 = jnp.max(l1, axis=1, keepdims=True)
    i1 = jnp.min(jnp.where(l1 == m1, lane, float(LANES)), axis=1, keepdims=True)
    l2 = jnp.where(lane == i1, -jnp.inf, l1)
    m2 = jnp.max(l2, axis=1, keepdims=True)
    i2 = jnp.min(jnp.where(l2 == m2, lane, float(LANES)), axis=1, keepdims=True)
    e2 = jnp.exp(m2 - m1)
    w1 = 1.0 / (1.0 + e2)
    w2 = e2 / (1.0 + e2)
    member = jnp.where((lane == i1) | (lane == i2), 1.0, 0.0)
    run = run_ref[0:1, :]
    rank = _dot(tri_ref[...], member.astype(BF16)) + run
    p1 = jnp.sum(jnp.where(lane == i1, rank, 0.0), axis=1, keepdims=True)
    p2 = jnp.sum(jnp.where(lane == i2, rank, 0.0), axis=1, keepdims=True)
    run = run + jnp.sum(member, axis=0, keepdims=True)
    run_ref[0:1, :] = run
    cnt_ref[...] = jnp.broadcast_to(run, cnt_ref.shape)
    meta = jnp.zeros(logits.shape, F32)
    for idx, val in enumerate([i1, i2, w1, w2, p1, p2]):
        meta = jnp.where(lane == float(idx), val, meta)
    meta_ref[...] = meta


def _router(x2, g, mod4, wr_hi, wr_lo, seq):
    n, d = x2.shape
    tm = 512
    per_b = seq // tm
    tri = jnp.asarray(np.tril(np.ones((tm, tm), np.float32), -1), BF16)
    mod_spec = lambda k: pl.BlockSpec((None, None, 1, d), lambda i: (i // per_b, k, 0, 0))
    return pl.pallas_call(
        _router_kernel,
        grid=(n // tm,),
        in_specs=[
            pl.BlockSpec((tm, d), lambda i: (i, 0)),
            pl.BlockSpec((1, d), lambda i: (0, 0)),
            mod_spec(4), mod_spec(3),
            pl.BlockSpec((d, 2 * LANES), lambda i: (0, 0)),
            pl.BlockSpec((tm, tm), lambda i: (0, 0)),
        ],
        out_specs=[
            pl.BlockSpec((tm, d // 2), lambda i: (i, 0)),
            pl.BlockSpec((tm, LANES), lambda i: (i, 0)),
            pl.BlockSpec((8, LANES), lambda i: (0, 0)),
        ],
        out_shape=[jax.ShapeDtypeStruct((n, d // 2), jnp.uint32),
                   jax.ShapeDtypeStruct((n, LANES), F32),
                   jax.ShapeDtypeStruct((8, LANES), F32)],
        scratch_shapes=[pltpu.VMEM((tm, d), F32), pltpu.VMEM((8, LANES), F32)],
        compiler_params=_cparams(("arbitrary",)),
        name="router",
    )(x2, g, mod4, mod4, jnp.concatenate([wr_hi, wr_lo], axis=1), tri)


MOE_TM = 1024
DISP_TM = 512


def _dispatch_kernel(dest_hbm, hp_ref, xs_in_hbm, xs_hbm, idx_ref, sem_i, sem_r):
    del xs_in_hbm
    i = pl.program_id(0)
    n_idx = 2 * DISP_TM
    cp = pltpu.make_async_copy(dest_hbm.at[pl.ds(pl.multiple_of(i * n_idx, n_idx), n_idx)], idx_ref, sem_i)
    cp.start()
    cp.wait()

    def row_copy(r, k):
        return pltpu.make_async_copy(hp_ref.at[pl.ds(r, 1), :],
                                     xs_hbm.at[pl.ds(idx_ref[k * DISP_TM + r], 1), :], sem_r)

    def start(r, c):
        row_copy(r, 0).start(priority=0)
        row_copy(r, 1).start(priority=1)
        return c

    def wait(r, c):
        row_copy(r, 0).wait()
        row_copy(r, 1).wait()
        return c

    lax.fori_loop(0, DISP_TM, start, 0, unroll=8)
    lax.fori_loop(0, DISP_TM, wait, 0, unroll=8)


def _dispatch(dest_tiles, hp, n_rows):
    n, w = hp.shape
    xs0 = jnp.zeros((n_rows, w), jnp.uint32)
    return pl.pallas_call(
        _dispatch_kernel,
        grid=(n // DISP_TM,),
        in_specs=[
            pl.BlockSpec(memory_space=pl.ANY),
            pl.BlockSpec((DISP_TM, w), lambda i: (i, 0)),
            pl.BlockSpec(memory_space=pl.ANY),
        ],
        out_specs=pl.BlockSpec(memory_space=pl.ANY),
        out_shape=jax.ShapeDtypeStruct((n_rows, w), jnp.uint32),
        scratch_shapes=[
            pltpu.SMEM((2 * DISP_TM,), I32),
            pltpu.SemaphoreType.DMA(()),
            pltpu.SemaphoreType.DMA(()),
        ],
        input_output_aliases={2: 0},
        compiler_params=_cparams(("arbitrary",)),
        name="moe_dispatch",
    )(dest_tiles, hp, xs0)


MOE_SUB = 256


def _moe_kernel(te_ref, nu_ref, tr_ref, xs_ref, wg_ref, wu_ref, wd_ref, o_ref, xb_ref):
    t = pl.program_id(0)
    f = pl.program_id(1)
    rows = tr_ref[t]

    @pl.when(f == 0)
    def _():
        o_ref[...] = jnp.zeros_like(o_ref)

    @pl.when((rows > 0) & (f == 0))
    def _():
        xb_ref[...] = _unpack_bf16_pairs(xs_ref[...])

    @pl.when(rows == MOE_TM)
    def _():
        o_ref[...] += _swiglu_tile(xb_ref[...], wg_ref, wu_ref, wd_ref)

    for sb in range(MOE_TM // MOE_SUB):
        @pl.when((rows < MOE_TM) & (rows > sb * MOE_SUB))
        def _(sb=sb):
            sl = slice(sb * MOE_SUB, (sb + 1) * MOE_SUB)
            o_ref[sl, :] += _swiglu_tile(xb_ref[sl, :], wg_ref, wu_ref, wd_ref)


def _moe(tile_expert, n_used, tile_rows, xs, wg, wu, wd):
    d = wg.shape[1]
    n_tiles = tile_expert.shape[0]
    dff = wg.shape[2]
    tf = 512
    nf = dff // tf

    def f_blk(t, f, nu):
        return jnp.where(t < nu[0], f, nf - 1)

    grid_spec = pltpu.PrefetchScalarGridSpec(
        num_scalar_prefetch=3,
        grid=(n_tiles, nf),
        in_specs=[
            pl.BlockSpec((MOE_TM, d // 2), lambda t, f, te, nu, tr: (jnp.minimum(t, nu[0] - 1), 0)),
            pl.BlockSpec((None, d, tf), lambda t, f, te, nu, tr: (te[t], 0, f_blk(t, f, nu))),
            pl.BlockSpec((None, d, tf), lambda t, f, te, nu, tr: (te[t], 0, f_blk(t, f, nu))),
            pl.BlockSpec((None, tf, d), lambda t, f, te, nu, tr: (te[t], f_blk(t, f, nu), 0)),
        ],
        out_specs=pl.BlockSpec((MOE_TM, d), lambda t, f, te, nu, tr: (t, 0)),
        scratch_shapes=[pltpu.VMEM((MOE_TM, d), BF16)],
    )
    return pl.pallas_call(
        _moe_kernel,
        grid_spec=grid_spec,
        out_shape=jax.ShapeDtypeStruct((n_tiles * MOE_TM, d), F32),
        compiler_params=_cparams(("arbitrary", "arbitrary")),
        name="moe_experts",
    )(tile_expert, n_used, tile_rows, xs, wg, wu, wd)


COMB_TM = 512


def _combine_kernel(dest_hbm, y_hbm, x_ref, meta_ref, gt_ref, o_ref, idx_ref, y_ref, sem_i, sem_r):
    i = pl.program_id(0)
    n_idx = 2 * COMB_TM

    def row_copy(slot, j):
        return pltpu.make_async_copy(
            y_hbm.at[pl.ds(idx_ref[slot * n_idx + j], 1), :],
            y_ref.at[slot, pl.ds(j, 1), :], sem_r.at[slot])

    def fetch(step, slot):
        cp = pltpu.make_async_copy(
            dest_hbm.at[pl.ds(pl.multiple_of(step * n_idx, n_idx), n_idx)],
            idx_ref.at[pl.ds(pl.multiple_of(slot * n_idx, n_idx), n_idx)], sem_i)
        cp.start()
        cp.wait()

        def start(t, c):
            row_copy(slot, 2 * t).start(priority=0)
            row_copy(slot, 2 * t + 1).start(priority=1)
            return c

        lax.fori_loop(0, n_idx // 2, start, 0, unroll=4)

    @pl.when(i == 0)
    def _():
        fetch(i, 0)

    for slot in range(2):
        @pl.when(i % 2 == slot)
        def _(slot=slot):
            @pl.when(i + 1 < pl.num_programs(0))
            def _():
                fetch(i + 1, 1 - slot)

            def wait(j, c):
                row_copy(slot, j).wait()
                return c

            lax.fori_loop(0, n_idx, wait, 0, unroll=8)
            w1 = meta_ref[:, 2:3]
            w2 = meta_ref[:, 3:4]
            y1 = y_ref[slot, 0:COMB_TM, :]
            y2 = y_ref[slot, COMB_TM:n_idx, :]
            o_ref[...] = x_ref[...] + gt_ref[...] * (w1 * y1 + w2 * y2)


def _combine(dest, y_sorted, x2, meta, mod4, seq):
    n, d = x2.shape
    tm = COMB_TM
    per_b = seq // tm
    return pl.pallas_call(
        _combine_kernel,
        grid=(n // tm,),
        in_specs=[
            pl.BlockSpec(memory_space=pl.ANY),
            pl.BlockSpec(memory_space=pl.ANY),
            pl.BlockSpec((tm, d), lambda i: (i, 0)),
            pl.BlockSpec((tm, LANES), lambda i: (i, 0)),
            pl.BlockSpec((None, None, 1, d), lambda i: (i // per_b, 5, 0, 0)),
        ],
        out_specs=pl.BlockSpec((tm, d), lambda i: (i, 0)),
        out_shape=jax.ShapeDtypeStruct((n, d), F32),
        scratch_shapes=[
            pltpu.SMEM((2 * 2 * tm,), I32),
            pltpu.VMEM((2, 2 * tm, d), F32),
            pltpu.SemaphoreType.DMA(()),
            pltpu.SemaphoreType.DMA((2,)),
        ],
        compiler_params=_cparams(("arbitrary",)),
        name="moe_combine",
    )(dest, y_sorted, x2, meta, mod4)


def _moe_layer(x2, g, mod4, w_router, wg, wu, wd, seq):
    n, d = x2.shape
    wr = jnp.zeros((d, LANES), F32).at[:, :N_EXPERTS].set(w_router)
    wr_hi = wr.astype(BF16)
    wr_lo = (wr - wr_hi.astype(F32)).astype(BF16)
    hp, meta, cnt = _router(x2, g, mod4, wr_hi, wr_lo, seq)

    n_tiles = 2 * n // MOE_TM + N_EXPERTS
    counts = cnt[0, :N_EXPERTS].astype(I32)
    tiles_per = (counts + MOE_TM - 1) // MOE_TM
    tile_end = jnp.cumsum(tiles_per)
    row_start = (tile_end - tiles_per) * MOE_TM
    n_used = tile_end[-1]
    idx = meta[:, 0:2].astype(I32)
    pos = meta[:, 4:6].astype(I32)
    start_of = sum(jnp.where(idx == e, row_start[e], 0) for e in range(N_EXPERTS))
    dest = start_of + pos
    all_tiles = jnp.arange(n_tiles, dtype=I32)
    tile_ids = jnp.minimum(all_tiles, n_used - 1)
    tile_expert = jnp.sum(tile_ids[:, None] >= tile_end[None, :], axis=1).astype(I32)
    onehot_e = tile_expert[:, None] == jnp.arange(N_EXPERTS, dtype=I32)[None, :]
    first_tile = jnp.sum(jnp.where(onehot_e, (tile_end - tiles_per)[None, :], 0), axis=1)
    cnt_tile = jnp.sum(jnp.where(onehot_e, counts[None, :], 0), axis=1)
    tile_rows = jnp.clip(cnt_tile - (all_tiles - first_tile) * MOE_TM, 0, MOE_TM)
    tile_rows = jnp.where(all_tiles < n_used, tile_rows, 0).astype(I32)
    assert COMB_TM == DISP_TM
    dest_tiles = dest.reshape(n // COMB_TM, COMB_TM, 2).transpose(0, 2, 1).reshape(-1)

    xs = _dispatch(dest_tiles, hp, n_tiles * MOE_TM)
    y_sorted = _moe(tile_expert, n_used.reshape(1).astype(I32), tile_rows, xs, wg, wu, wd)
    return _combine(dest_tiles, y_sorted, x2, meta, mod4, seq)


def _t5_bucket(dist):
    max_exact = N_BUCKETS // 2
    d = np.maximum(dist, 1).astype(np.float32)
    large = max_exact + (np.log(d / max_exact) / np.log(MAX_DIST / max_exact)
                         * (N_BUCKETS - max_exact)).astype(np.int32)
    large = np.minimum(large, N_BUCKETS - 1)
    return np.where(dist < max_exact, dist, large).astype(np.int32)


def _pack_w_in(w):
    splits = [M_HEADS * M_QK, M_HEADS * M_QK, M_HEADS * M_V, M_HEADS * M_V, 2 * M_HEADS,
              G_HEADS * G_QK, G_HEADS * G_QK, G_HEADS * G_V, G_HEADS * G_V, G_RANK,
              A_HEADS * A_HD, A_KV * A_HD, A_KV * A_HD]
    offs = np.concatenate([[0], np.cumsum(splits)])
    seg = lambda i: w[:, offs[i]:offs[i + 1]]
    main = jnp.concatenate([seg(i) for i in (0, 1, 2, 3, 5, 6, 7, 8, 10, 11, 12)], axis=1)
    small = jnp.concatenate(
        [seg(4), seg(9), jnp.zeros((w.shape[0], LANES - 2 * M_HEADS - G_RANK), w.dtype)], axis=1)
    return main.astype(BF16), small.astype(BF16)


def kernel(x, c, w_ada, b_ada, g_mix_norm, g_ffn_norm, w_in, b_gates_m, w_conv_m, b_conv_m, g_out_m,
           w_gla_decay, b_gla_decay, g_out_g, g_qnorm, g_knorm, sinks, rel_bias, w_out, w_ffn_gate,
           w_ffn_up, w_ffn_down, w_router, w_moe_gate, w_moe_up, w_moe_down):
    bsz, seq, d = x.shape
    depth = w_ada.shape[0]
    n = bsz * seq
    assert seq % 1024 == 0 and d == 2048 and n % MOE_TM == 0 and bsz % max(MLSTM_NB, GLA_NB, SWA_NB) == 0

    mod = _ada(c, w_ada, b_ada)

    jj = np.arange(BLK)[:, None]
    ss = np.arange(2 * BLK)[None, :]
    buckets = _t5_bucket(np.clip(jj + BLK - ss, 0, None))
    onehot = jnp.asarray(np.eye(N_BUCKETS, dtype=np.float32)[buckets.reshape(-1)])
    bias = jnp.einsum("pb,bh->hp", onehot, rel_bias.astype(F32),
                      precision=lax.Precision.HIGHEST).reshape(A_HEADS, BLK, 2 * BLK)
    in_window = (jj + BLK - ss >= 0) & (jj + BLK - ss < WINDOW)
    masks = np.stack([in_window & (ss >= BLK), in_window])
    bias = jnp.where(jnp.asarray(masks)[:, None], (bias * LOG2E)[None], -jnp.inf)
    ind = jnp.asarray(np.kron(np.eye(LANES // A_HD), np.ones((A_HD, A_HD))) / A_HD, BF16)

    x2 = x.reshape(n, d)
    for l in range(depth):
        mod4 = mod[l].reshape(bsz, 6, 1, d)
        w_main, w_small = _pack_w_in(w_in[l])
        pm, ps = _inproj(x2, g_mix_norm[l].reshape(1, d), mod4, w_main, w_small, seq)
        pm = pm.reshape(bsz, seq, -1)
        ps = ps.reshape(bsz, seq, -1)

        gb = jnp.zeros((1, LANES), F32).at[0, :2 * M_HEADS].set(b_gates_m[l].reshape(-1))
        ym = _mlstm(pm, ps, w_conv_m[l], b_conv_m[l].reshape(1, -1), gb, g_out_m[l].reshape(1, -1))
        wdec = jnp.zeros((LANES, G_HEADS * G_QK), F32).at[S_GA:S_GA + G_RANK].set(w_gla_decay[l])
        yg = _gla(pm, ps, wdec, b_gla_decay[l].reshape(1, -1), g_out_g[l].reshape(1, -1))
        gq = jnp.tile(g_qnorm[l], A_HEADS).reshape(1, -1) * (A_HD ** -0.5 * LOG2E)
        gk = jnp.tile(g_knorm[l], A_KV).reshape(1, -1)
        ya = _swa(pm, bias, gq, gk, sinks[l] * LOG2E, ind)
        x2 = _outproj(ym.reshape(n, -1), yg.reshape(n, -1), ya.reshape(n, -1), w_out[l].astype(BF16),
                      x2, mod4, seq)

        gf = g_ffn_norm[l].reshape(1, d)
        if l % 2 == 0:
            j = l // 2
            x2 = _ffn(x2, gf, mod4, w_ffn_gate[j], w_ffn_up[j], w_ffn_down[j], seq)
        else:
            j = l // 2
            x2 = _moe_layer(x2, gf, mod4, w_router[j], w_moe_gate[j], w_moe_up[j], w_moe_down[j], seq)
    return x2.reshape(bsz, seq, d)
```
